```python
import jax, jax.numpy as jnp
from jax import lax
import numpy as np

D_MODEL = 1024
BATCH = 16
SEQ = 2048
DEPTH = 1

N_Q_HEADS = 8
N_KV_HEADS = 2
HEAD_DIM = 64
ATTN_WIDTH = N_Q_HEADS * HEAD_DIM
KV_WIDTH = N_KV_HEADS * HEAD_DIM
WINDOW = 128
BLOCK = 128
ROPE_THETA = 500000.0
ROPE_DIM = HEAD_DIM // 4
CONV_WIDTH = D_MODEL - ATTN_WIDTH
CONV_GROUPS = 8
CONV_WIDTH_K = 31
MIX_WIDTH = ATTN_WIDTH + CONV_WIDTH
IN_WIDTH = ATTN_WIDTH + 2 * KV_WIDTH + 2 * CONV_WIDTH
PEER_HEADS = 8
N_KEYS = 128
N_EXPERTS = N_KEYS * N_KEYS
PEER_QUERY_DIM = 256
PEER_HALF = PEER_QUERY_DIM // 2
PEER_TOPK = 16
PEER_CHUNK = 128
PLE_DIM = 256
DEEPNORM_ALPHA = (2 * DEPTH) ** 0.25
DEEPNORM_BETA = (8 * DEPTH) ** -0.25
LN_EPS = 1e-5
NEG_INF = -1e30

kernel_name = "hymba_conv_swa_sink_peer_deepnorm_ple"


def layer_norm(x, g, b):
    xf = x.astype(jnp.float32)
    mu = jnp.mean(xf, axis=-1, keepdims=True)
    var = jnp.mean(jnp.square(xf - mu), axis=-1, keepdims=True)
    y = (xf - mu) * lax.rsqrt(var + LN_EPS) * g.astype(jnp.float32) + b.astype(jnp.float32)
    return y.astype(x.dtype)


def partial_rotary(t, positions):
    half = ROPE_DIM // 2
    inv_freq = ROPE_THETA ** (-jnp.arange(half, dtype=jnp.float32) * (2.0 / ROPE_DIM))
    ang = positions.astype(jnp.float32)[..., None] * inv_freq
    cos = jnp.cos(ang)[:, :, None, :]
    sin = jnp.sin(ang)[:, :, None, :]
    tr = t[..., :ROPE_DIM].astype(jnp.float32)
    t1, t2 = tr[..., :half], tr[..., half:]
    rot = jnp.concatenate([t1 * cos - t2 * sin, t2 * cos + t1 * sin], axis=-1).astype(t.dtype)
    return jnp.concatenate([rot, t[..., ROPE_DIM:]], axis=-1)


def sliding_window_attention(q, k, v, sinks):
    b, s = q.shape[0], q.shape[1]
    nb = s // BLOCK
    g = N_Q_HEADS // N_KV_HEADS
    qb = q.reshape(b, nb, BLOCK, N_KV_HEADS, g, HEAD_DIM)

    def band(t):
        tp = jnp.pad(t, ((0, 0), (BLOCK, 0), (0, 0), (0, 0)))
        tp = tp.reshape(b, nb + 1, BLOCK, N_KV_HEADS, HEAD_DIM)
        return jnp.concatenate([tp[:, :-1], tp[:, 1:]], axis=2)

    kb, vb = band(k), band(v)
    scores = jnp.einsum('bnqkgd,bnskd->bnkgqs', qb, kb).astype(jnp.float32) * (HEAD_DIM ** -0.5)
    qi = jnp.arange(BLOCK)[:, None]
    si = jnp.arange(2 * BLOCK)[None, :]
    diff = qi + BLOCK - si
    blk = jnp.arange(nb)[:, None, None]
    valid = (diff >= 0) & (diff < WINDOW) & (blk * BLOCK - BLOCK + si[None] >= 0)
    scores = jnp.where(valid[None, :, None, None], scores, NEG_INF)
    sink = sinks.astype(jnp.float32).reshape(1, 1, N_KV_HEADS, g, 1, 1)
    m = jnp.maximum(jnp.max(scores, axis=-1, keepdims=True), sink)
    e = jnp.exp(scores - m)
    denom = jnp.sum(e, axis=-1, keepdims=True) + jnp.exp(sink - m)
    probs = (e / denom).astype(v.dtype)
    out = jnp.einsum('bnkgqs,bnskd->bnqkgd', probs, vb)
    return out.reshape(b, s, ATTN_WIDTH)


def conv_group(a, gate, conv_w, conv_b, ln_g, ln_b):
    h = a * jax.nn.sigmoid(gate)
    h = lax.conv_general_dilated(
        h, conv_w[:, None, :].astype(h.dtype), window_strides=(1,),
        padding=((CONV_WIDTH_K - 1, 0),),
        dimension_numbers=('NWC', 'WIO', 'NWC'),
        feature_group_count=CONV_WIDTH) + conv_b
    h = layer_norm(h, ln_g, ln_b)
    return jax.nn.silu(h)


def peer(x, w_query, sub_keys, u_table, v_table):
    b, s, d = x.shape
    t = b * s
    xt = x.reshape(t, d)
    q = (xt @ w_query).reshape(t, PEER_HEADS, 2, PEER_HALF)
    sc = jnp.einsum('thpc,hpnc->thpn', q, sub_keys).astype(jnp.float32)
    top_s, top_i = lax.top_k(sc, PEER_TOPK)
    cand = top_s[:, :, 0, :, None] + top_s[:, :, 1, None, :]
    best_s, best_c = lax.top_k(cand.reshape(t, PEER_HEADS, PEER_TOPK * PEER_TOPK), PEER_TOPK)
    i1 = jnp.take_along_axis(top_i[:, :, 0], best_c // PEER_TOPK, axis=-1)
    i2 = jnp.take_along_axis(top_i[:, :, 1], best_c % PEER_TOPK, axis=-1)
    n_chunks = t // PEER_CHUNK
    ids = (i1 * N_KEYS + i2).reshape(n_chunks, PEER_CHUNK, PEER_HEADS * PEER_TOPK)
    gates = jax.nn.softmax(best_s, axis=-1).astype(x.dtype).reshape(n_chunks, PEER_CHUNK, PEER_HEADS * PEER_TOPK)
    xc = xt.reshape(n_chunks, PEER_CHUNK, d)

    def chunk(args):
        xk, idk, gk = args
        u = jnp.take(u_table, idk, axis=0)
        h = jnp.einsum('cd,ced->ce', xk, u)
        act = jax.nn.gelu(h, approximate=False) * gk
        return jnp.einsum('ce,ced->cd', act, jnp.take(v_table, idk, axis=0))

    out = lax.map(chunk, (xc, ids, gates))
    return out.reshape(b, s, d)


def setup_inputs(seed: int = 0) -> dict:
    key = jax.random.key(seed)
    ks = jax.random.split(key, 24)
    f32 = jnp.float32
    nrm = lambda k, shape, scale: jax.random.normal(k, shape, f32) * scale
    x = jax.random.normal(ks[0], (BATCH, SEQ, D_MODEL), f32)
    p = jax.random.normal(ks[1], (DEPTH, BATCH, SEQ, PLE_DIM), f32)
    offsets = jax.random.randint(ks[2], (BATCH,), 0, 1024, dtype=jnp.int32)
    positions = (offsets[:, None] + jnp.arange(SEQ, dtype=jnp.int32)[None, :]).astype(jnp.int32)
    return {
        "x": x,
        "p": p,
        "positions": positions,
        "w_in": nrm(ks[3], (DEPTH, D_MODEL, IN_WIDTH), D_MODEL ** -0.5),
        "sinks": nrm(ks[4], (DEPTH, N_Q_HEADS), 0.5),
        "conv_w": nrm(ks[5], (DEPTH, CONV_WIDTH_K, CONV_WIDTH), CONV_WIDTH_K ** -0.5),
        "conv_b": nrm(ks[6], (DEPTH, CONV_WIDTH), 0.01),
        "conv_ln_g": 1.0 + nrm(ks[7], (DEPTH, CONV_WIDTH), 0.02),
        "conv_ln_b": nrm(ks[8], (DEPTH, CONV_WIDTH), 0.01),
        "w_out": nrm(ks[9], (DEPTH, MIX_WIDTH, D_MODEL), DEEPNORM_BETA * MIX_WIDTH ** -0.5),
        "ln1_g": 1.0 + nrm(ks[10], (DEPTH, D_MODEL), 0.02),
        "ln1_b": nrm(ks[11], (DEPTH, D_MODEL), 0.01),
        "peer_query": nrm(ks[12], (DEPTH, D_MODEL, PEER_HEADS * PEER_QUERY_DIM), D_MODEL ** -0.5),
        "peer_keys": nrm(ks[13], (DEPTH, PEER_HEADS, 2, N_KEYS, PEER_HALF), PEER_HALF ** -0.5),
        "peer_u": nrm(ks[14], (DEPTH, N_EXPERTS, D_MODEL), D_MODEL ** -0.5),
        "peer_v": nrm(ks[15], (DEPTH, N_EXPERTS, D_MODEL), DEEPNORM_BETA * PEER_HEADS ** -0.5),
        "ple_proj": nrm(ks[16], (DEPTH, PLE_DIM, D_MODEL), DEEPNORM_BETA * PLE_DIM ** -0.5),
        "ple_gate": nrm(ks[17], (DEPTH, D_MODEL, D_MODEL), D_MODEL ** -0.5),
        "ln2_g": 1.0 + nrm(ks[18], (DEPTH, D_MODEL), 0.02),
        "ln2_b": nrm(ks[19], (DEPTH, D_MODEL), 0.01),
    }


def reference(x, p, positions, w_in, sinks, conv_w, conv_b, conv_ln_g, conv_ln_b, w_out,
              ln1_g, ln1_b, peer_query, peer_keys, peer_u, peer_v, ple_proj, ple_gate,
              ln2_g, ln2_b):
    b, s, _ = x.shape
    splits = [ATTN_WIDTH, ATTN_WIDTH + KV_WIDTH, ATTN_WIDTH + 2 * KV_WIDTH,
              ATTN_WIDTH + 2 * KV_WIDTH + CONV_WIDTH]
    for i in range(DEPTH):
        h = x @ w_in[i]
        q, k, v, ca, cg = jnp.split(h, splits, axis=-1)
        q = partial_rotary(q.reshape(b, s, N_Q_HEADS, HEAD_DIM), positions)
        k = partial_rotary(k.reshape(b, s, N_KV_HEADS, HEAD_DIM), positions)
        v = v.reshape(b, s, N_KV_HEADS, HEAD_DIM)
        attn = sliding_window_attention(q, k, v, sinks[i])
        conv = conv_group(ca, cg, conv_w[i], conv_b[i], conv_ln_g[i], conv_ln_b[i])
        mixed = jnp.concatenate([attn, conv], axis=-1) @ w_out[i]
        x = layer_norm(DEEPNORM_ALPHA * x + mixed, ln1_g[i], ln1_b[i])
        r = DEEPNORM_ALPHA * x + peer(x, peer_query[i], peer_keys[i], peer_u[i], peer_v[i])
        e = jax.nn.sigmoid(r @ ple_gate[i]) * (p[i] @ ple_proj[i])
        x = layer_norm(r + e, ln2_g[i], ln2_b[i])
    return x
```

```python
import functools
import math

import jax
import jax.numpy as jnp
from jax import lax
from jax.experimental import pallas as pl
from jax.experimental.pallas import tpu as pltpu

F32 = jnp.float32
BF16 = jnp.bfloat16

D_MODEL = 1024
N_Q_HEADS = 8
N_KV_HEADS = 2
HEAD_DIM = 64
ATTN_WIDTH = N_Q_HEADS * HEAD_DIM
KV_WIDTH = N_KV_HEADS * HEAD_DIM
WINDOW = 128
BLOCK = 128
ROPE_THETA = 500000.0
ROPE_DIM = HEAD_DIM // 4
ROPE_HALF = ROPE_DIM // 2
CONV_WIDTH = D_MODEL - ATTN_WIDTH
CONV_TAPS = 31
PEER_HEADS = 8
N_KEYS = 128
N_EXPERTS = N_KEYS * N_KEYS
PEER_HALF = 128
PEER_TOPK = 16
PLE_DIM = 256
DEPTH = 1
DEEPNORM_ALPHA = (2 * DEPTH) ** 0.25
LN_EPS = 1e-5
NEG_INF = -1e30

LANES = 128
SUBLANES = 8
VMEM_LIMIT_BYTES = 56 * 1024 * 1024

RANK_CODE_BASE = -2.0e38
RANK_CODE_STEP = 1.0e36
RANK_CODE_LIMIT = -1.9e38
PAD_SCORE = -1.5e38


def _compiler_params(semantics):
    return pltpu.CompilerParams(dimension_semantics=semantics,
                                vmem_limit_bytes=VMEM_LIMIT_BYTES)


def _layer_norm(v, g, b):
    mu = jnp.mean(v, axis=-1, keepdims=True)
    d = v - mu
    var = jnp.mean(d * d, axis=-1, keepdims=True)
    return d * lax.rsqrt(var + LN_EPS) * g + b


QK_GROUPS = (ATTN_WIDTH + 2 * KV_WIDTH) // LANES


def _inproj_kernel(x_ref, w_ref, pos_ref, invf_ref, q_ref, k_ref, v_ref, glu_ref):
    h = jnp.dot(x_ref[...].astype(BF16), w_ref[...], preferred_element_type=F32)
    tm = h.shape[0]
    ang = pos_ref[...] * invf_ref[...]
    lane = lax.broadcasted_iota(jnp.int32, (tm, LANES), 1) % HEAD_DIM
    first = lane < ROPE_HALF
    second = jnp.logical_and(lane >= ROPE_HALF, lane < ROPE_DIM)
    cos = jnp.cos(ang)
    sin = jnp.sin(ang)
    sin_signed = jnp.where(first, -sin, jnp.where(second, sin, 0.0))

    def rotate(t):
        up = pltpu.roll(t, LANES - ROPE_HALF, 1)
        dn = pltpu.roll(t, ROPE_HALF, 1)
        return t * cos + jnp.where(first, up, dn) * sin_signed

    scale = HEAD_DIM ** -0.5
    for g in range(ATTN_WIDTH // LANES):
        q_ref[:, g * LANES:(g + 1) * LANES] = (
            rotate(h[:, g * LANES:(g + 1) * LANES]) * scale).astype(BF16)
    o = ATTN_WIDTH
    for g in range(2 * KV_WIDTH // LANES):
        k_ref[:, g * LANES:(g + 1) * LANES] = rotate(
            h[:, o + g * LANES:o + (g + 1) * LANES]).astype(BF16)
    o += 2 * KV_WIDTH
    v_ref[...] = h[:, o:o + 2 * KV_WIDTH].astype(BF16)
    o += 2 * KV_WIDTH
    glu_ref[...] = h[:, o:o + CONV_WIDTH] * jax.nn.sigmoid(
        h[:, o + CONV_WIDTH:o + 2 * CONV_WIDTH])


def _in_proj(x2, w_cat, pos, invf, tm):
    t = x2.shape[0]
    n_in = w_cat.shape[1]
    row = lambda i: (i, 0)
    fixed = lambda i: (0, 0)
    return pl.pallas_call(
        _inproj_kernel,
        grid=(t // tm,),
        in_specs=[pl.BlockSpec((tm, D_MODEL), row),
                  pl.BlockSpec((D_MODEL, n_in), fixed),
                  pl.BlockSpec((tm, 1), row),
                  pl.BlockSpec((1, LANES), fixed)],
        out_specs=[pl.BlockSpec((tm, ATTN_WIDTH), row),
                   pl.BlockSpec((tm, 2 * KV_WIDTH), row),
                   pl.BlockSpec((tm, 2 * KV_WIDTH), row),
                   pl.BlockSpec((tm, CONV_WIDTH), row)],
        out_shape=[jax.ShapeDtypeStruct((t, ATTN_WIDTH), BF16),
                   jax.ShapeDtypeStruct((t, 2 * KV_WIDTH), BF16),
                   jax.ShapeDtypeStruct((t, 2 * KV_WIDTH), BF16),
                   jax.ShapeDtypeStruct((t, CONV_WIDTH), F32)],
        compiler_params=_compiler_params(("parallel",)),
        name="in_proj",
    )(x2, w_cat, pos, invf)


def _attn_kernel(sink_ref, q_ref, k_ref, v_ref, o_ref):
    s_len = q_ref.shape[1]
    nb = s_len // BLOCK
    qi = lax.broadcasted_iota(jnp.int32, (BLOCK, 2 * BLOCK), 0)
    si = lax.broadcasted_iota(jnp.int32, (BLOCK, 2 * BLOCK), 1)
    band = jnp.logical_and(si > qi, si <= qi + WINDOW)
    lane = lax.broadcasted_iota(jnp.int32, (2 * BLOCK, LANES), 1)
    low = lane < HEAD_DIM
    heads_per_group = N_Q_HEADS // N_KV_HEADS
    zero = jnp.zeros((), BF16)

    def body(n, carry):
        start = pl.multiple_of(n * BLOCK, BLOCK)
        prev = pl.multiple_of(jnp.maximum(n - 1, 0) * BLOCK, BLOCK)
        valid = jnp.logical_and(band, si >= jnp.where(n == 0, BLOCK, 0))
        q = q_ref[0, pl.ds(start, BLOCK), :]
        k2 = jnp.concatenate([k_ref[0, pl.ds(prev, BLOCK), :],
                              k_ref[0, pl.ds(start, BLOCK), :]], axis=0)
        v2 = jnp.concatenate([v_ref[0, pl.ds(prev, BLOCK), :],
                              v_ref[0, pl.ds(start, BLOCK), :]], axis=0)
        for pair in range(N_Q_HEADS // 2):
            g = (2 * pair) // heads_per_group
            qp = q[:, pair * LANES:(pair + 1) * LANES]
            kg = k2[:, g * LANES:(g + 1) * LANES]
            vg = v2[:, g * LANES:(g + 1) * LANES]
            out = jnp.zeros((BLOCK, LANES), F32)
            for half in range(2):
                keep = low if half == 0 else jnp.logical_not(low)
                kx = jnp.where(keep, kg, zero)
                vx = jnp.where(keep, vg, zero)
                sink = sink_ref[2 * pair + half]
                s = lax.dot_general(qp, kx, (((1,), (1,)), ((), ())),
                                    preferred_element_type=F32)
                s = jnp.where(valid, s, NEG_INF)
                m = jnp.maximum(jnp.max(s, axis=-1, keepdims=True), sink)
                e = jnp.exp(s - m)
                denom = jnp.sum(e, axis=-1, keepdims=True) + jnp.exp(sink - m)
                pv = jnp.dot(e.astype(BF16), vx, preferred_element_type=F32)
                out = out + pv * (1.0 / denom)
            o_ref[0, pl.ds(start, BLOCK), pair * LANES:(pair + 1) * LANES] = out.astype(BF16)
        return carry

    lax.fori_loop(0, nb, body, 0)


def _attention(sinks, q3, k3, v3):
    b, s, _ = q3.shape
    blk = lambda w: pl.BlockSpec((1, s, w), lambda i: (i, 0, 0))
    return pl.pallas_call(
        _attn_kernel,
        grid=(b,),
        in_specs=[pl.BlockSpec(memory_space=pltpu.SMEM),
                  blk(ATTN_WIDTH), blk(2 * KV_WIDTH), blk(2 * KV_WIDTH)],
        out_specs=blk(ATTN_WIDTH),
        out_shape=jax.ShapeDtypeStruct((b, s, ATTN_WIDTH), BF16),
        compiler_params=_compiler_params(("parallel",)),
        name="attention",
    )(sinks, q3, k3, v3)


CONV_PAD = 32
CONV_ROWS = 128
CONV_COPY_ROWS = 256


def _conv_kernel(h_ref, w_ref, b_ref, g_ref, beta_ref, o_ref, sh_ref, y_ref):
    s_len = h_ref.shape[1]
    padded = CONV_PAD + s_len
    for l in range(CONV_WIDTH // LANES):
        cols = slice(l * LANES, (l + 1) * LANES)
        sh_ref[0, 0:CONV_PAD, :] = jnp.zeros((CONV_PAD, LANES), F32)
        sh_ref[0, CONV_PAD:, :] = h_ref[0, :, cols]
        for r in range(1, SUBLANES):
            for lo in range(SUBLANES, padded, CONV_COPY_ROWS):
                hi = min(lo + CONV_COPY_ROWS, padded)
                sh_ref[r, lo:hi, :] = sh_ref[0, lo - r:hi - r, :]

        def body(c, carry):
            base = pl.multiple_of(CONV_PAD + c * CONV_ROWS, SUBLANES)
            acc = jnp.zeros((CONV_ROWS, LANES), F32) + b_ref[:, cols]
            for j in range(CONV_TAPS):
                a, r = divmod(CONV_TAPS - 1 - j, SUBLANES)
                acc = acc + sh_ref[r, pl.ds(base - SUBLANES * a, CONV_ROWS), :] * w_ref[j:j + 1, cols]
            y_ref[pl.ds(pl.multiple_of(c * CONV_ROWS, SUBLANES), CONV_ROWS), cols] = acc
            return carry

        lax.fori_loop(0, s_len // CONV_ROWS, body, 0)

    def norm(c, carry):
        rows = pl.ds(pl.multiple_of(c * CONV_ROWS, SUBLANES), CONV_ROWS)
        y = _layer_norm(y_ref[rows, :], g_ref[...], beta_ref[...])
        o_ref[0, rows, :] = (y * jax.nn.sigmoid(y)).astype(BF16)
        return carry

    lax.fori_loop(0, s_len // CONV_ROWS, norm, 0)


def _conv_group(h3, conv_w, conv_b, ln_g, ln_b):
    b, s, _ = h3.shape
    blk = pl.BlockSpec((1, s, CONV_WIDTH), lambda i: (i, 0, 0))
    vec = pl.BlockSpec((1, CONV_WIDTH), lambda i: (0, 0))
    return pl.pallas_call(
        _conv_kernel,
        grid=(b,),
        in_specs=[blk, pl.BlockSpec((CONV_TAPS, CONV_WIDTH), lambda i: (0, 0)), vec, vec, vec],
        out_specs=blk,
        out_shape=jax.ShapeDtypeStruct((b, s, CONV_WIDTH), BF16),
        scratch_shapes=[pltpu.VMEM((SUBLANES, CONV_PAD + s, LANES), F32),
                        pltpu.VMEM((s, CONV_WIDTH), F32)],
        compiler_params=_compiler_params(("parallel",)),
        name="conv_group",
    )(h3, conv_w, conv_b, ln_g, ln_b)


def _outproj_kernel(att_ref, cnv_ref, x_ref, wa_ref, wc_ref, g_ref, b_ref, y_ref, yb_ref):
    mixed = jnp.dot(att_ref[...], wa_ref[...], preferred_element_type=F32)
    mixed = mixed + jnp.dot(cnv_ref[...], wc_ref[...], preferred_element_type=F32)
    y = _layer_norm(DEEPNORM_ALPHA * x_ref[...] + mixed, g_ref[...], b_ref[...])
    y_ref[...] = y
    yb_ref[...] = y.astype(BF16)


def _out_proj(att, cnv, x2, wa, wc, g, b, tm):
    t = x2.shape[0]
    row = lambda i: (i, 0)
    fixed = lambda i: (0, 0)
    return pl.pallas_call(
        _outproj_kernel,
        grid=(t // tm,),
        in_specs=[pl.BlockSpec((tm, ATTN_WIDTH), row), pl.BlockSpec((tm, CONV_WIDTH), row),
                  pl.BlockSpec((tm, D_MODEL), row),
                  pl.BlockSpec((ATTN_WIDTH, D_MODEL), fixed),
                  pl.BlockSpec((CONV_WIDTH, D_MODEL), fixed),
                  pl.BlockSpec((1, D_MODEL), fixed), pl.BlockSpec((1, D_MODEL), fixed)],
        out_specs=[pl.BlockSpec((tm, D_MODEL), row), pl.BlockSpec((tm, D_MODEL), row)],
        out_shape=[jax.ShapeDtypeStruct((t, D_MODEL), F32),
                   jax.ShapeDtypeStruct((t, D_MODEL), BF16)],
        compiler_params=_compiler_params(("parallel",)),
        name="out_proj",
    )(att, cnv, x2, wa, wc, g, b)


N_GROUPS = 2 * PEER_HEADS
KEY_VREGS = N_KEYS // SUBLANES
PAIR_LIMIT = tuple(PEER_TOPK // (a + 1) for a in range(PEER_TOPK))


def _extract_topk(vals, bases, sub, rounds):
    vals = list(vals)
    subf = sub.astype(F32)
    tops = []
    for a in range(rounds):
        best = vals[0]
        best_base = jnp.full(best.shape, float(bases[0]), F32)
        for k in range(1, len(vals)):
            better = vals[k] > best
            best = jnp.where(better, vals[k], best)
            best_base = jnp.where(better, float(bases[k]), best_base)
        top = jnp.max(best, axis=0, keepdims=True)
        order = jnp.where(best == top, best_base + subf, 1.0e9)
        winner = jnp.min(order, axis=0, keepdims=True)
        winner_base = winner - subf
        code = RANK_CODE_BASE - a * RANK_CODE_STEP
        for k in range(len(vals)):
            vals[k] = jnp.where(winner_base == float(bases[k]), code, vals[k])
        tops.append(top)
    return vals, tops


def _routing_kernel(x_ref, wq_ref, keys_ref, e1_ref, c1_ref, e2_ref, r2_ref,
                    qb_ref, sc_ref, mk_ref, top_ref):
    tt = x_ref.shape[0]
    n_chunks = tt // LANES
    qb_ref[...] = jnp.dot(x_ref[...], wq_ref[...], preferred_element_type=F32).astype(BF16)
    for g in range(N_GROUPS):
        sc_ref[g] = lax.dot_general(keys_ref[g], qb_ref[:, g * PEER_HALF:(g + 1) * PEER_HALF],
                                    (((1,), (1,)), ((), ())), preferred_element_type=F32)

    sub = lax.broadcasted_iota(jnp.int32, (SUBLANES, LANES), 0)
    key_bases = tuple(SUBLANES * k for k in range(KEY_VREGS))

    def level1(it, carry):
        g = it // n_chunks
        col = pl.multiple_of((it % n_chunks) * LANES, LANES)
        vals = [sc_ref[g, SUBLANES * k:SUBLANES * (k + 1), pl.ds(col, LANES)]
                for k in range(KEY_VREGS)]
        marked, tops = _extract_topk(vals, key_bases, sub, PEER_TOPK)
        for k in range(KEY_VREGS):
            mk_ref[g, SUBLANES * k:SUBLANES * (k + 1), pl.ds(col, LANES)] = marked[k]
        for a in range(PEER_TOPK):
            top_ref[g, a:a + 1, pl.ds(col, LANES)] = tops[a]
        return carry

    lax.fori_loop(0, N_GROUPS * n_chunks, level1, 0)

    pair_bases = (0, SUBLANES) + tuple(PEER_TOPK * a for a in range(1, PEER_TOPK))

    def pair_vregs(v1_rows, v2_lo, v2_hi):
        cands = [v1_rows[0] + v2_lo, v1_rows[0] + v2_hi]
        for a in range(1, PEER_TOPK):
            cands.append(jnp.where(sub < PAIR_LIMIT[a], v1_rows[a] + v2_lo, PAD_SCORE))
        return cands

    def level2(it, carry):
        h = it // n_chunks
        col = pl.multiple_of((it % n_chunks) * LANES, LANES)
        g1 = 2 * h
        g2 = 2 * h + 1
        v1_rows = [top_ref[g1, a:a + 1, pl.ds(col, LANES)] for a in range(PEER_TOPK)]
        v2_lo = top_ref[g2, 0:SUBLANES, pl.ds(col, LANES)]
        v2_hi = top_ref[g2, SUBLANES:2 * SUBLANES, pl.ds(col, LANES)]
        cands = pair_vregs(v1_rows, v2_lo, v2_hi)
        marked, tops = _extract_topk(cands, pair_bases, sub, PEER_TOPK)
        best = tops[0]
        counts = []
        z = jnp.zeros((1, LANES), F32)
        for j, cand in enumerate(cands):
            chosen = marked[j] < RANK_CODE_LIMIT
            z = z + jnp.sum(jnp.where(chosen, jnp.exp(cand - best), 0.0), axis=0, keepdims=True)
            cnt = jnp.sum(jnp.where(chosen, 1.0, 0.0), axis=0, keepdims=True)
            if j == 1:
                counts[0] = counts[0] + cnt
            else:
                counts.append(cnt)
        inv_z = 1.0 / z
        m1 = v1_rows[0]
        m2 = top_ref[g2, 0:1, pl.ds(col, LANES)]
        for k in range(KEY_VREGS):
            rows = slice(SUBLANES * k, SUBLANES * (k + 1))
            mk1 = mk_ref[g1, rows, pl.ds(col, LANES)]
            cnt1 = jnp.zeros((SUBLANES, LANES), F32)
            for a in range(PEER_TOPK):
                cnt1 = jnp.where(mk1 == RANK_CODE_BASE - a * RANK_CODE_STEP, counts[a], cnt1)
            c1_ref[h, rows, pl.ds(col, LANES)] = cnt1
            e1_ref[h, rows, pl.ds(col, LANES)] = jnp.exp(
                sc_ref[g1, rows, pl.ds(col, LANES)] - m1) * inv_z
            mk2 = mk_ref[g2, rows, pl.ds(col, LANES)]
            rank2 = jnp.floor((RANK_CODE_BASE - mk2) * (1.0 / RANK_CODE_STEP) + 0.5)
            r2_ref[h, rows, pl.ds(col, LANES)] = jnp.where(
                mk2 < RANK_CODE_LIMIT, rank2, float(PEER_TOPK))
            e2_ref[h, rows, pl.ds(col, LANES)] = jnp.exp(
                sc_ref[g2, rows, pl.ds(col, LANES)] - m2)
        return carry

    lax.fori_loop(0, PEER_HEADS * n_chunks, level2, 0)


def _peer_routing(x1b, wq, keys, tt):
    t = x1b.shape[0]
    out_blk = pl.BlockSpec((PEER_HEADS, N_KEYS, tt), lambda i: (0, 0, i))
    out_sds = jax.ShapeDtypeStruct((PEER_HEADS, N_KEYS, t), F32)
    return pl.pallas_call(
        _routing_kernel,
        grid=(t // tt,),
        in_specs=[pl.BlockSpec((tt, D_MODEL), lambda i: (i, 0)),
                  pl.BlockSpec((D_MODEL, N_GROUPS * PEER_HALF), lambda i: (0, 0)),
                  pl.BlockSpec((N_GROUPS, N_KEYS, PEER_HALF), lambda i: (0, 0, 0))],
        out_specs=[out_blk] * 4,
        out_shape=[out_sds] * 4,
        scratch_shapes=[pltpu.VMEM((tt, N_GROUPS * PEER_HALF), BF16),
                        pltpu.VMEM((N_GROUPS, N_KEYS, tt), F32),
                        pltpu.VMEM((N_GROUPS, N_KEYS, tt), F32),
                        pltpu.VMEM((N_GROUPS, PEER_TOPK, tt), F32)],
        compiler_params=_compiler_params(("parallel",)),
        name="peer_routing",
    )(x1b, wq, keys)


def _dense_kernel(x_ref, u_ref, vt_ref, e1_ref, c1_ref, e2_ref, r2_ref, o_ref,
                  acc_ref, ht_ref, act_ref):
    j = pl.program_id(1)
    eb, tt = ht_ref.shape
    rows_per_step = eb // N_KEYS

    @pl.when(j == 0)
    def _():
        acc_ref[...] = jnp.zeros_like(acc_ref)

    ht_ref[...] = lax.dot_general(u_ref[...], x_ref[...], (((1,), (1,)), ((), ())),
                                  preferred_element_type=F32)
    for il in range(rows_per_step):
        for tc in range(tt // LANES):
            cols = slice(tc * LANES, (tc + 1) * LANES)
            gate = jnp.zeros((N_KEYS, LANES), F32)
            for h in range(PEER_HEADS):
                c1 = c1_ref[h, il:il + 1, cols]
                e1 = e1_ref[h, il:il + 1, cols]
                gate = gate + jnp.where(r2_ref[h, :, cols] < c1, e2_ref[h, :, cols] * e1, 0.0)
            hh = ht_ref[il * N_KEYS:(il + 1) * N_KEYS, cols]
            gelu = 0.5 * hh * (1.0 + lax.erf(hh * (2.0 ** -0.5)))
            act_ref[il * N_KEYS:(il + 1) * N_KEYS, cols] = (gelu * gate).astype(BF16)
    acc_ref[...] += jnp.dot(vt_ref[...], act_ref[...], preferred_element_type=F32)

    @pl.when(j == pl.num_programs(1) - 1)
    def _():
        o_ref[...] = acc_ref[...].T


def _peer_dense(x1b, u, vt, e1, c1, e2, r2, tt, eb):
    t = x1b.shape[0]
    route = pl.BlockSpec((PEER_HEADS, N_KEYS, tt), lambda i, j: (0, 0, i))
    route_rows = pl.BlockSpec((PEER_HEADS, eb // N_KEYS, tt), lambda i, j: (0, j, i))
    return pl.pallas_call(
        _dense_kernel,
        grid=(t // tt, N_EXPERTS // eb),
        in_specs=[pl.BlockSpec((tt, D_MODEL), lambda i, j: (i, 0)),
                  pl.BlockSpec((eb, D_MODEL), lambda i, j: (j, 0)),
                  pl.BlockSpec((D_MODEL, eb), lambda i, j: (0, j)),
                  route_rows, route_rows, route, route],
        out_specs=pl.BlockSpec((tt, D_MODEL), lambda i, j: (i, 0)),
        out_shape=jax.ShapeDtypeStruct((t, D_MODEL), F32),
        scratch_shapes=[pltpu.VMEM((D_MODEL, tt), F32),
                        pltpu.VMEM((eb, tt), F32),
                        pltpu.VMEM((eb, tt), BF16)],
        compiler_params=_compiler_params(("parallel", "arbitrary")),
        name="peer_dense",
    )(x1b, u, vt, e1, c1, e2, r2)


def _final_kernel(x1_ref, peer_ref, p_ref, wg_ref, wp_ref, g_ref, b_ref, o_ref):
    r = DEEPNORM_ALPHA * x1_ref[...] + peer_ref[...]
    gate = jax.nn.sigmoid(jnp.dot(r.astype(BF16), wg_ref[...], preferred_element_type=F32))
    emb = jnp.dot(p_ref[...].astype(BF16), wp_ref[...], preferred_element_type=F32)
    o_ref[...] = _layer_norm(r + gate * emb, g_ref[...], b_ref[...])


def _final(x1, peer, p2, wg, wp, g, b, tm):
    t = x1.shape[0]
    row = lambda i: (i, 0)
    fixed = lambda i: (0, 0)
    return pl.pallas_call(
        _final_kernel,
        grid=(t // tm,),
        in_specs=[pl.BlockSpec((tm, D_MODEL), row), pl.BlockSpec((tm, D_MODEL), row),
                  pl.BlockSpec((tm, PLE_DIM), row),
                  pl.BlockSpec((D_MODEL, D_MODEL), fixed), pl.BlockSpec((PLE_DIM, D_MODEL), fixed),
                  pl.BlockSpec((1, D_MODEL), fixed), pl.BlockSpec((1, D_MODEL), fixed)],
        out_specs=pl.BlockSpec((tm, D_MODEL), row),
        out_shape=jax.ShapeDtypeStruct((t, D_MODEL), F32),
        compiler_params=_compiler_params(("parallel",)),
        name="final",
    )(x1, peer, p2, wg, wp, g, b)


ROW_TILE = 512
ROUTE_TILE = 256
DENSE_TOKENS = 512
DENSE_EXPERTS = 1024


def _rope_lane_table():
    inv_freq = ROPE_THETA ** (-jnp.arange(ROPE_HALF, dtype=F32) * (2.0 / ROPE_DIM))
    lane = jnp.arange(LANES) % HEAD_DIM
    return jnp.where(lane < ROPE_DIM, inv_freq[lane % ROPE_HALF], 0.0).reshape(1, LANES).astype(F32)


def _dup_kv(w):
    h0, h1 = w[:, :HEAD_DIM], w[:, HEAD_DIM:]
    return jnp.concatenate([h0, h0, h1, h1], axis=1)


def kernel(x, p, positions, w_in, sinks, conv_w, conv_b, conv_ln_g, conv_ln_b, w_out, ln1_g, ln1_b,
           peer_query, peer_keys, peer_u, peer_v, ple_proj, ple_gate, ln2_g, ln2_b):
    b, s, d = x.shape
    t = b * s
    pos = positions.reshape(t, 1).astype(F32)
    invf = _rope_lane_table()
    xc = x.reshape(t, d)
    vec = lambda a: a.reshape(1, -1)
    for i in range(DEPTH):
        w = w_in[i]
        o = ATTN_WIDTH
        w_cat = jnp.concatenate(
            [w[:, :o], _dup_kv(w[:, o:o + KV_WIDTH]), _dup_kv(w[:, o + KV_WIDTH:o + 2 * KV_WIDTH]),
             w[:, o + 2 * KV_WIDTH:]], axis=1).astype(BF16)
        q, k, v, glu = _in_proj(xc, w_cat, pos, invf, ROW_TILE)
        att = _attention(sinks[i], q.reshape(b, s, -1), k.reshape(b, s, -1), v.reshape(b, s, -1))
        cnv = _conv_group(glu.reshape(b, s, -1), conv_w[i], vec(conv_b[i]),
                          vec(conv_ln_g[i]), vec(conv_ln_b[i]))
        wo = w_out[i].astype(BF16)
        x1, x1b = _out_proj(att.reshape(t, -1), cnv.reshape(t, -1), xc, wo[:ATTN_WIDTH],
                            wo[ATTN_WIDTH:], vec(ln1_g[i]), vec(ln1_b[i]), ROW_TILE)
        keys = peer_keys[i].reshape(N_GROUPS, N_KEYS, PEER_HALF).astype(BF16)
        e1, c1, e2, r2 = _peer_routing(x1b, peer_query[i].astype(BF16), keys, ROUTE_TILE)
        peer = _peer_dense(x1b, peer_u[i].astype(BF16), peer_v[i].astype(BF16).T, e1, c1, e2, r2,
                           DENSE_TOKENS, DENSE_EXPERTS)
        xc = _final(x1, peer, p[i].reshape(t, -1), ple_gate[i].astype(BF16),
                    ple_proj[i].astype(BF16), vec(ln2_g[i]), vec(ln2_b[i]), ROW_TILE)
    return xc.reshape(b, s, d)
```

```python
import jax
import jax.numpy as jnp
from jax import lax
from jax.experimental import pallas as pl
from jax.experimental.pallas import tpu as pltpu

F32 = jnp.float32
BF16 = jnp.bfloat16

D_MODEL = 1024
N_Q_HEADS = 8
N_KV_HEADS = 2
HEAD_DIM = 64
ATTN_WIDTH = N_Q_HEADS * HEAD_DIM
KV_WIDTH = N_KV_HEADS * HEAD_DIM
WINDOW = 128
BLOCK = 128
ROPE_THETA = 500000.0
ROPE_DIM = HEAD_DIM // 4
ROPE_HALF = ROPE_DIM // 2
CONV_WIDTH = D_MODEL - ATTN_WIDTH
CONV_TAPS = 31
PEER_HEADS = 8
N_KEYS = 128
N_EXPERTS = N_KEYS * N_KEYS
PEER_HALF = 128
PEER_TOPK = 16
PLE_DIM = 256
DEPTH = 1
DEEPNORM_ALPHA = (2 * DEPTH) ** 0.25
LN_EPS = 1e-5
NEG_INF = -1e30

LANES = 128
SUBLANES = 8
VMEM_LIMIT_BYTES = 56 * 1024 * 1024

RANK_CODE_BASE = -(2.0 ** 127)
RANK_CODE_STEP = 2.0 ** 120
RANK_CODE_LIMIT = -1.5 * 2.0 ** 126
PAD_SCORE = -(2.0 ** 126)


def _compiler_params(semantics, flags=None):
    return pltpu.CompilerParams(dimension_semantics=semantics,
                                vmem_limit_bytes=VMEM_LIMIT_BYTES, flags=flags)


def _layer_norm(v, g, b):
    mu = jnp.mean(v, axis=-1, keepdims=True)
    d = v - mu
    var = jnp.mean(d * d, axis=-1, keepdims=True)
    return d * lax.rsqrt(var + LN_EPS) * g + b


QK_GROUPS = (ATTN_WIDTH + 2 * KV_WIDTH) // LANES


def _inproj_kernel(x_ref, w_ref, pos_ref, invf_ref, q_ref, k_ref, v_ref, glu_ref):
    h = jnp.dot(x_ref[...].astype(BF16), w_ref[...], preferred_element_type=F32)
    tm = h.shape[0]
    ang = pos_ref[...] * invf_ref[...]
    lane = lax.broadcasted_iota(jnp.int32, (tm, LANES), 1) % HEAD_DIM
    first = lane < ROPE_HALF
    second = jnp.logical_and(lane >= ROPE_HALF, lane < ROPE_DIM)
    cos = jnp.cos(ang)
    sin = jnp.sin(ang)
    sin_signed = jnp.where(first, -sin, jnp.where(second, sin, 0.0))

    def rotate(t):
        up = pltpu.roll(t, LANES - ROPE_HALF, 1)
        dn = pltpu.roll(t, ROPE_HALF, 1)
        return t * cos + jnp.where(first, up, dn) * sin_signed

    scale = HEAD_DIM ** -0.5
    for g in range(ATTN_WIDTH // LANES):
        q_ref[:, g * LANES:(g + 1) * LANES] = (
            rotate(h[:, g * LANES:(g + 1) * LANES]) * scale).astype(BF16)
    o = ATTN_WIDTH
    for g in range(2 * KV_WIDTH // LANES):
        k_ref[:, g * LANES:(g + 1) * LANES] = rotate(
            h[:, o + g * LANES:o + (g + 1) * LANES]).astype(BF16)
    o += 2 * KV_WIDTH
    v_ref[...] = h[:, o:o + 2 * KV_WIDTH].astype(BF16)
    o += 2 * KV_WIDTH
    glu_ref[...] = h[:, o:o + CONV_WIDTH] * jax.nn.sigmoid(
        h[:, o + CONV_WIDTH:o + 2 * CONV_WIDTH])


def _in_proj(x2, w_cat, pos, invf, tm):
    t = x2.shape[0]
    n_in = w_cat.shape[1]
    row = lambda i: (i, 0)
    fixed = lambda i: (0, 0)
    return pl.pallas_call(
        _inproj_kernel,
        grid=(t // tm,),
        in_specs=[pl.BlockSpec((tm, D_MODEL), row),
                  pl.BlockSpec((D_MODEL, n_in), fixed),
                  pl.BlockSpec((tm, 1), row),
                  pl.BlockSpec((1, LANES), fixed)],
        out_specs=[pl.BlockSpec((tm, ATTN_WIDTH), row),
                   pl.BlockSpec((tm, 2 * KV_WIDTH), row),
                   pl.BlockSpec((tm, 2 * KV_WIDTH), row),
                   pl.BlockSpec((tm, CONV_WIDTH), row)],
        out_shape=[jax.ShapeDtypeStruct((t, ATTN_WIDTH), BF16),
                   jax.ShapeDtypeStruct((t, 2 * KV_WIDTH), BF16),
                   jax.ShapeDtypeStruct((t, 2 * KV_WIDTH), BF16),
                   jax.ShapeDtypeStruct((t, CONV_WIDTH), F32)],
        compiler_params=_compiler_params(("parallel",)),
        name="in_proj",
    )(x2, w_cat, pos, invf)


def _attn_kernel(sink_ref, q_ref, k_ref, v_ref, o_ref):
    s_len = q_ref.shape[1]
    nb = s_len // BLOCK
    qi = lax.broadcasted_iota(jnp.int32, (BLOCK, 2 * BLOCK), 0)
    si = lax.broadcasted_iota(jnp.int32, (BLOCK, 2 * BLOCK), 1)
    band = jnp.logical_and(si > qi, si <= qi + WINDOW)
    lane = lax.broadcasted_iota(jnp.int32, (2 * BLOCK, LANES), 1)
    low = lane < HEAD_DIM
    heads_per_group = N_Q_HEADS // N_KV_HEADS
    zero = jnp.zeros((), BF16)

    def body(n, carry):
        start = pl.multiple_of(n * BLOCK, BLOCK)
        prev = pl.multiple_of(jnp.maximum(n - 1, 0) * BLOCK, BLOCK)
        valid = jnp.logical_and(band, si >= jnp.where(n == 0, BLOCK, 0))
        q = q_ref[0, pl.ds(start, BLOCK), :]
        k2 = jnp.concatenate([k_ref[0, pl.ds(prev, BLOCK), :],
                              k_ref[0, pl.ds(start, BLOCK), :]], axis=0)
        v2 = jnp.concatenate([v_ref[0, pl.ds(prev, BLOCK), :],
                              v_ref[0, pl.ds(start, BLOCK), :]], axis=0)
        for pair in range(N_Q_HEADS // 2):
            g = (2 * pair) // heads_per_group
            qp = q[:, pair * LANES:(pair + 1) * LANES]
            kg = k2[:, g * LANES:(g + 1) * LANES]
            vg = v2[:, g * LANES:(g + 1) * LANES]
            out = jnp.zeros((BLOCK, LANES), F32)
            for half in range(2):
                keep = low if half == 0 else jnp.logical_not(low)
                kx = jnp.where(keep, kg, zero)
                vx = jnp.where(keep, vg, zero)
                sink = sink_ref[2 * pair + half]
                s = lax.dot_general(qp, kx, (((1,), (1,)), ((), ())),
                                    preferred_element_type=F32)
                s = jnp.where(valid, s, NEG_INF)
                m = jnp.maximum(jnp.max(s, axis=-1, keepdims=True), sink)
                e = jnp.exp(s - m)
                denom = jnp.sum(e, axis=-1, keepdims=True) + jnp.exp(sink - m)
                pv = jnp.dot(e.astype(BF16), vx, preferred_element_type=F32)
                out = out + pv * (1.0 / denom)
            o_ref[0, pl.ds(start, BLOCK), pair * LANES:(pair + 1) * LANES] = out.astype(BF16)
        return carry

    lax.fori_loop(0, nb, body, 0)


def _attention(sinks, q3, k3, v3):
    b, s, _ = q3.shape
    blk = lambda w: pl.BlockSpec((1, s, w), lambda i: (i, 0, 0))
    return pl.pallas_call(
        _attn_kernel,
        grid=(b,),
        in_specs=[pl.BlockSpec(memory_space=pltpu.SMEM),
                  blk(ATTN_WIDTH), blk(2 * KV_WIDTH), blk(2 * KV_WIDTH)],
        out_specs=blk(ATTN_WIDTH),
        out_shape=jax.ShapeDtypeStruct((b, s, ATTN_WIDTH), BF16),
        compiler_params=_compiler_params(("parallel",)),
        name="attention",
    )(sinks, q3, k3, v3)


CONV_PAD = 32
CONV_ROWS = 128
CONV_COPY_ROWS = 256


def _conv_kernel(h_ref, w_ref, b_ref, g_ref, beta_ref, o_ref, sh_ref, y_ref):
    s_len = h_ref.shape[1]
    padded = CONV_PAD + s_len
    for l in range(CONV_WIDTH // LANES):
        cols = slice(l * LANES, (l + 1) * LANES)
        sh_ref[0, 0:CONV_PAD, :] = jnp.zeros((CONV_PAD, LANES), F32)
        sh_ref[0, CONV_PAD:, :] = h_ref[0, :, cols]
        for r in range(1, SUBLANES):
            for lo in range(SUBLANES, padded, CONV_COPY_ROWS):
                hi = min(lo + CONV_COPY_ROWS, padded)
                sh_ref[r, lo:hi, :] = sh_ref[0, lo - r:hi - r, :]

        def body(c, carry):
            base = pl.multiple_of(CONV_PAD + c * CONV_ROWS, SUBLANES)
            acc = jnp.zeros((CONV_ROWS, LANES), F32) + b_ref[:, cols]
            for j in range(CONV_TAPS):
                a, r = divmod(CONV_TAPS - 1 - j, SUBLANES)
                acc = acc + sh_ref[r, pl.ds(base - SUBLANES * a, CONV_ROWS), :] * w_ref[j:j + 1, cols]
            y_ref[pl.ds(pl.multiple_of(c * CONV_ROWS, SUBLANES), CONV_ROWS), cols] = acc
            return carry

        lax.fori_loop(0, s_len // CONV_ROWS, body, 0)

    def norm(c, carry):
        rows = pl.ds(pl.multiple_of(c * CONV_ROWS, SUBLANES), CONV_ROWS)
        y = _layer_norm(y_ref[rows, :], g_ref[...], beta_ref[...])
        o_ref[0, rows, :] = (y * jax.nn.sigmoid(y)).astype(BF16)
        return carry

    lax.fori_loop(0, s_len // CONV_ROWS, norm, 0)


def _conv_group(h3, conv_w, conv_b, ln_g, ln_b):
    b, s, _ = h3.shape
    blk = pl.BlockSpec((1, s, CONV_WIDTH), lambda i: (i, 0, 0))
    vec = pl.BlockSpec((1, CONV_WIDTH), lambda i: (0, 0))
    return pl.pallas_call(
        _conv_kernel,
        grid=(b,),
        in_specs=[blk, pl.BlockSpec((CONV_TAPS, CONV_WIDTH), lambda i: (0, 0)), vec, vec, vec],
        out_specs=blk,
        out_shape=jax.ShapeDtypeStruct((b, s, CONV_WIDTH), BF16),
        scratch_shapes=[pltpu.VMEM((SUBLANES, CONV_PAD + s, LANES), F32),
                        pltpu.VMEM((s, CONV_WIDTH), F32)],
        compiler_params=_compiler_params(("parallel",)),
        name="conv_group",
    )(h3, conv_w, conv_b, ln_g, ln_b)


def _outproj_kernel(att_ref, cnv_ref, x_ref, wa_ref, wc_ref, g_ref, b_ref, y_ref, yb_ref, ybt_ref):
    mixed = jnp.dot(att_ref[...], wa_ref[...], preferred_element_type=F32)
    mixed = mixed + jnp.dot(cnv_ref[...], wc_ref[...], preferred_element_type=F32)
    y = _layer_norm(DEEPNORM_ALPHA * x_ref[...] + mixed, g_ref[...], b_ref[...])
    y_ref[...] = y
    yb_ref[...] = y.astype(BF16)
    ybt_ref[...] = y.T.astype(BF16)


def _out_proj(att, cnv, x2, wa, wc, g, b, tm):
    t = x2.shape[0]
    row = lambda i: (i, 0)
    fixed = lambda i: (0, 0)
    return pl.pallas_call(
        _outproj_kernel,
        grid=(t // tm,),
        in_specs=[pl.BlockSpec((tm, ATTN_WIDTH), row), pl.BlockSpec((tm, CONV_WIDTH), row),
                  pl.BlockSpec((tm, D_MODEL), row),
                  pl.BlockSpec((ATTN_WIDTH, D_MODEL), fixed),
                  pl.BlockSpec((CONV_WIDTH, D_MODEL), fixed),
                  pl.BlockSpec((1, D_MODEL), fixed), pl.BlockSpec((1, D_MODEL), fixed)],
        out_specs=[pl.BlockSpec((tm, D_MODEL), row), pl.BlockSpec((tm, D_MODEL), row),
                   pl.BlockSpec((D_MODEL, tm), lambda i: (0, i))],
        out_shape=[jax.ShapeDtypeStruct((t, D_MODEL), F32),
                   jax.ShapeDtypeStruct((t, D_MODEL), BF16),
                   jax.ShapeDtypeStruct((D_MODEL, t), BF16)],
        compiler_params=_compiler_params(("parallel",)),
        name="out_proj",
    )(att, cnv, x2, wa, wc, g, b)


N_GROUPS = 2 * PEER_HEADS
KEY_VREGS = N_KEYS // SUBLANES


def _tree_argmax(vals, bases):
    items = [(v, float(b)) for v, b in zip(vals, bases)]
    while len(items) > 1:
        nxt = []
        for i in range(0, len(items) - 1, 2):
            (va, ba), (vb, bb) = items[i], items[i + 1]
            better = vb > va
            nxt.append((jnp.where(better, vb, va), jnp.where(better, bb, ba)))
        if len(items) % 2:
            nxt.append(items[-1])
        items = nxt
    return items[0]


def _extract_topk(vals, bases, subf, rounds):
    vals = list(vals)
    tops = []
    for a in range(rounds):
        best, best_base = _tree_argmax(vals, bases)
        top = jnp.max(best, axis=0, keepdims=True)
        order = jnp.where(best == top, best_base + subf, 1.0e9)
        winner = jnp.min(order, axis=0, keepdims=True)
        winner_base = winner - subf
        code = RANK_CODE_BASE - a * RANK_CODE_STEP
        vals = [jnp.where(winner_base == float(bases[k]), code, vals[k])
                for k in range(len(vals))]
        tops.append(top)
    return vals, tops


PAIR_VREGS = ((0, 0, 8, 0), (0, 8, 8, 8), (1, 0, 8, 16), (2, 0, 5, 32), (3, 0, 4, 48),
              (4, 0, 3, 64), (5, 0, 2, 80), (6, 0, 2, 96), (7, 0, 2, 112))
PAIR_TAIL_BASE = 128


def _routing_kernel(x_ref, wq_ref, keys_ref, e1_ref, c1_ref, e2_ref, r2_ref,
                    qb_ref, sc_ref, mk_ref, top_ref):
    tt = x_ref.shape[0]
    n_chunks = tt // LANES
    qb_ref[...] = jnp.dot(x_ref[...], wq_ref[...], preferred_element_type=F32).astype(BF16)
    for g in range(N_GROUPS):
        sc_ref[g] = lax.dot_general(keys_ref[g], qb_ref[:, g * PEER_HALF:(g + 1) * PEER_HALF],
                                    (((1,), (1,)), ((), ())), preferred_element_type=F32)

    sub = lax.broadcasted_iota(jnp.int32, (SUBLANES, LANES), 0)
    subf = sub.astype(F32)
    key_bases = tuple(SUBLANES * k for k in range(KEY_VREGS))

    def level1(g, carry):
        for c in range(n_chunks):
            cols = slice(c * LANES, (c + 1) * LANES)
            vals = [sc_ref[g, SUBLANES * k:SUBLANES * (k + 1), cols] for k in range(KEY_VREGS)]
            marked, tops = _extract_topk(vals, key_bases, subf, PEER_TOPK)
            for k in range(KEY_VREGS):
                mk_ref[g, SUBLANES * k:SUBLANES * (k + 1), cols] = marked[k]
            for a in range(PEER_TOPK):
                top_ref[g, a:a + 1, cols] = tops[a]
        return carry

    lax.fori_loop(0, N_GROUPS, level1, 0)

    pair_bases = tuple(v[3] for v in PAIR_VREGS) + (PAIR_TAIL_BASE,)

    def level2(h, carry):
        g1 = 2 * h
        g2 = 2 * h + 1
        for c in range(n_chunks):
            cols = slice(c * LANES, (c + 1) * LANES)
            v2 = (top_ref[g2, 0:SUBLANES, cols], top_ref[g2, SUBLANES:2 * SUBLANES, cols])
            m1 = top_ref[g1, 0:1, cols]
            m2 = top_ref[g2, 0:1, cols]
            cands = []
            for a, b0, used, _ in PAIR_VREGS:
                cand = top_ref[g1, a:a + 1, cols] + v2[b0 // SUBLANES]
                cands.append(cand if used == SUBLANES else jnp.where(sub < used, cand, PAD_SCORE))
            cands.append(top_ref[g1, SUBLANES:2 * SUBLANES, cols] + m2)
            marked, _ = _extract_topk(cands, pair_bases, subf, PEER_TOPK)
            best = m1 + m2
            picked = [m < RANK_CODE_LIMIT for m in marked]
            ones = [jnp.where(p, 1.0, 0.0) for p in picked]
            zsum = jnp.zeros((SUBLANES, LANES), F32)
            for p, cand in zip(picked, cands):
                zsum = zsum + jnp.where(p, jnp.exp(cand - best), 0.0)
            inv_z = 0.5 / jnp.sum(zsum, axis=0, keepdims=True)
            widths = [jnp.sum(ones[0] + ones[1], axis=0, keepdims=True)]
            widths += [jnp.sum(ones[j], axis=0, keepdims=True) for j in range(2, len(PAIR_VREGS))]
            widths += [ones[-1][s:s + 1, :] for s in range(SUBLANES)]
            widths = [jnp.broadcast_to(w, (SUBLANES, LANES)) for w in widths]
            for kk in range(KEY_VREGS // 2):
                pair_rows = slice(2 * SUBLANES * kk, 2 * SUBLANES * (kk + 1))
                e2_pair, r2_pair = [], []
                for k in (2 * kk, 2 * kk + 1):
                    rows = slice(SUBLANES * k, SUBLANES * (k + 1))
                    mk1 = mk_ref[g1, rows, cols]
                    cnt1 = jnp.zeros((SUBLANES, LANES), F32)
                    for a in range(PEER_TOPK):
                        cnt1 = jnp.where(mk1 == RANK_CODE_BASE - a * RANK_CODE_STEP, widths[a], cnt1)
                    c1_ref[h, rows, cols] = cnt1
                    e1_ref[h, rows, cols] = jnp.exp(sc_ref[g1, rows, cols] - m1) * inv_z
                    mk2 = mk_ref[g2, rows, cols]
                    rank2 = jnp.floor((RANK_CODE_BASE - mk2) * (1.0 / RANK_CODE_STEP) + 0.5)
                    r2_pair.append(jnp.where(mk2 < RANK_CODE_LIMIT, rank2, float(PEER_TOPK)))
                    e2_pair.append(jnp.exp(sc_ref[g2, rows, cols] - m2))
                r2_ref[h, pair_rows, cols] = jnp.concatenate(r2_pair, axis=0).astype(BF16)
                e2_ref[h, pair_rows, cols] = jnp.concatenate(e2_pair, axis=0).astype(BF16)
        return carry

    lax.fori_loop(0, PEER_HEADS, level2, 0)


def _peer_routing(x1b, wq, keys, tt):
    t = x1b.shape[0]
    out_blk = pl.BlockSpec((PEER_HEADS, N_KEYS, tt), lambda i: (0, 0, i))
    sds = lambda dt: jax.ShapeDtypeStruct((PEER_HEADS, N_KEYS, t), dt)
    return pl.pallas_call(
        _routing_kernel,
        grid=(t // tt,),
        in_specs=[pl.BlockSpec((tt, D_MODEL), lambda i: (i, 0)),
                  pl.BlockSpec((D_MODEL, N_GROUPS * PEER_HALF), lambda i: (0, 0)),
                  pl.BlockSpec((N_GROUPS, N_KEYS, PEER_HALF), lambda i: (0, 0, 0))],
        out_specs=[out_blk] * 4,
        out_shape=[sds(F32), sds(F32), sds(BF16), sds(BF16)],
        scratch_shapes=[pltpu.VMEM((tt, N_GROUPS * PEER_HALF), BF16),
                        pltpu.VMEM((N_GROUPS, N_KEYS, tt), F32),
                        pltpu.VMEM((N_GROUPS, N_KEYS, tt), F32),
                        pltpu.VMEM((N_GROUPS, PEER_TOPK, tt), F32)],
        compiler_params=_compiler_params(("parallel",)),
        name="peer_routing",
    )(x1b, wq, keys)


DENSE_SUB = 512
BF16_ROWS = 2 * SUBLANES
DENSE_FLAGS = None


def _dense_kernel(xt_ref, u_ref, vt_ref, e1_ref, c1_ref, e2_in_ref, r2_in_ref, o_ref,
                  acc_ref, ht_ref, act_ref, e2_ref, r2_ref):
    j = pl.program_id(1)
    last = pl.num_programs(1) - 1
    _, eb, tt = ht_ref.shape
    n_sub = eb // DENSE_SUB
    zero = jnp.zeros((), BF16)
    write_slot = j % 2
    read_slot = 1 - write_slot

    def sub_rows(sb):
        return slice(sb * DENSE_SUB, (sb + 1) * DENSE_SUB)

    def stage_a(sb):
        ht_ref[write_slot, sub_rows(sb), :] = jnp.dot(u_ref[sub_rows(sb), :], xt_ref[...],
                                                      preferred_element_type=F32)

    def stage_b(sb):
        for il in range(sb * DENSE_SUB // N_KEYS, (sb + 1) * DENSE_SUB // N_KEYS):
            for tc in range(tt // LANES):
                cols = slice(tc * LANES, (tc + 1) * LANES)
                c1 = [jnp.broadcast_to(c1_ref[h, il:il + 1, cols], (BF16_ROWS, LANES)).astype(BF16)
                      for h in range(PEER_HEADS)]
                e1 = [jnp.broadcast_to(e1_ref[h, il:il + 1, cols], (BF16_ROWS, LANES)).astype(BF16)
                      for h in range(PEER_HEADS)]
                for p in range(N_KEYS // BF16_ROWS):
                    keys = slice(p * BF16_ROWS, (p + 1) * BF16_ROWS)
                    rows = slice(il * N_KEYS + p * BF16_ROWS, il * N_KEYS + (p + 1) * BF16_ROWS)
                    gate = jnp.zeros((BF16_ROWS, LANES), BF16)
                    for h in range(PEER_HEADS):
                        picked = jnp.where(r2_ref[h, keys, cols] < c1[h], e2_ref[h, keys, cols], zero)
                        gate = gate + picked * e1[h]
                    hh = ht_ref[read_slot, rows, cols]
                    gelu2 = hh * (1.0 + lax.erf(hh * (2.0 ** -0.5)))
                    act_ref[rows, cols] = gelu2.astype(BF16) * gate
        acc_ref[...] += jnp.dot(vt_ref[:, sub_rows(sb)], act_ref[sub_rows(sb), :],
                                preferred_element_type=F32)

    @pl.when(j == 0)
    def _():
        acc_ref[...] = jnp.zeros_like(acc_ref)
        e2_ref[...] = e2_in_ref[...]
        r2_ref[...] = r2_in_ref[...]
        for sb in range(n_sub):
            stage_a(sb)

    @pl.when(jnp.logical_and(j > 0, j < last))
    def _():
        stage_a(0)
        for sb in range(n_sub):
            if sb + 1 < n_sub:
                stage_a(sb + 1)
            stage_b(sb)

    @pl.when(j == last)
    def _():
        for sb in range(n_sub):
            stage_b(sb)
        o_ref[...] = acc_ref[...].T


def _peer_dense(x1bt, u, vt, e1, c1, e2, r2, tt, eb):
    t = x1bt.shape[1]
    n_blocks = N_EXPERTS // eb
    route = pl.BlockSpec((PEER_HEADS, N_KEYS, tt), lambda i, j: (0, 0, i))
    route_rows = pl.BlockSpec((PEER_HEADS, eb // N_KEYS, tt),
                              lambda i, j: (0, jnp.maximum(j - 1, 0), i))
    return pl.pallas_call(
        _dense_kernel,
        grid=(t // tt, n_blocks + 1),
        in_specs=[pl.BlockSpec((D_MODEL, tt), lambda i, j: (0, i)),
                  pl.BlockSpec((eb, D_MODEL), lambda i, j: (jnp.minimum(j, n_blocks - 1), 0)),
                  pl.BlockSpec((D_MODEL, eb), lambda i, j: (0, jnp.maximum(j - 1, 0))),
                  route_rows, route_rows, route, route],
        out_specs=pl.BlockSpec((tt, D_MODEL), lambda i, j: (i, 0)),
        out_shape=jax.ShapeDtypeStruct((t, D_MODEL), F32),
        scratch_shapes=[pltpu.VMEM((D_MODEL, tt), F32),
                        pltpu.VMEM((2, eb, tt), F32),
                        pltpu.VMEM((eb, tt), BF16),
                        pltpu.VMEM((PEER_HEADS, N_KEYS, tt), BF16),
                        pltpu.VMEM((PEER_HEADS, N_KEYS, tt), BF16)],
        compiler_params=_compiler_params(("parallel", "arbitrary"), DENSE_FLAGS),
        name="peer_dense",
    )(x1bt, u, vt, e1, c1, e2, r2)


def _final_kernel(x1_ref, peer_ref, p_ref, wg_ref, wp_ref, g_ref, b_ref, o_ref):
    r = DEEPNORM_ALPHA * x1_ref[...] + peer_ref[...]
    gate = jax.nn.sigmoid(jnp.dot(r.astype(BF16), wg_ref[...], preferred_element_type=F32))
    emb = jnp.dot(p_ref[...].astype(BF16), wp_ref[...], preferred_element_type=F32)
    o_ref[...] = _layer_norm(r + gate * emb, g_ref[...], b_ref[...])


def _final(x1, peer, p2, wg, wp, g, b, tm):
    t = x1.shape[0]
    row = lambda i: (i, 0)
    fixed = lambda i: (0, 0)
    return pl.pallas_call(
        _final_kernel,
        grid=(t // tm,),
        in_specs=[pl.BlockSpec((tm, D_MODEL), row), pl.BlockSpec((tm, D_MODEL), row),
                  pl.BlockSpec((tm, PLE_DIM), row),
                  pl.BlockSpec((D_MODEL, D_MODEL), fixed), pl.BlockSpec((PLE_DIM, D_MODEL), fixed),
                  pl.BlockSpec((1, D_MODEL), fixed), pl.BlockSpec((1, D_MODEL), fixed)],
        out_specs=pl.BlockSpec((tm, D_MODEL), row),
        out_shape=jax.ShapeDtypeStruct((t, D_MODEL), F32),
        compiler_params=_compiler_params(("parallel",)),
        name="final",
    )(x1, peer, p2, wg, wp, g, b)


ROW_TILE = 512
ROUTE_TILE = 256
DENSE_TOKENS = 512
DENSE_EXPERTS = 1024


def _rope_lane_table():
    inv_freq = ROPE_THETA ** (-jnp.arange(ROPE_HALF, dtype=F32) * (2.0 / ROPE_DIM))
    lane = jnp.arange(LANES) % HEAD_DIM
    return jnp.where(lane < ROPE_DIM, inv_freq[lane % ROPE_HALF], 0.0).reshape(1, LANES).astype(F32)


def _dup_kv(w):
    h0, h1 = w[:, :HEAD_DIM], w[:, HEAD_DIM:]
    return jnp.concatenate([h0, h0, h1, h1], axis=1)


def kernel(x, p, positions, w_in, sinks, conv_w, conv_b, conv_ln_g, conv_ln_b, w_out, ln1_g, ln1_b,
           peer_query, peer_keys, peer_u, peer_v, ple_proj, ple_gate, ln2_g, ln2_b):
    b, s, d = x.shape
    t = b * s
    pos = positions.reshape(t, 1).astype(F32)
    invf = _rope_lane_table()
    xc = x.reshape(t, d)
    vec = lambda a: a.reshape(1, -1)
    for i in range(DEPTH):
        w = w_in[i]
        o = ATTN_WIDTH
        w_cat = jnp.concatenate(
            [w[:, :o], _dup_kv(w[:, o:o + KV_WIDTH]), _dup_kv(w[:, o + KV_WIDTH:o + 2 * KV_WIDTH]),
             w[:, o + 2 * KV_WIDTH:]], axis=1).astype(BF16)
        q, k, v, glu = _in_proj(xc, w_cat, pos, invf, ROW_TILE)
        att = _attention(sinks[i], q.reshape(b, s, -1), k.reshape(b, s, -1), v.reshape(b, s, -1))
        cnv = _conv_group(glu.reshape(b, s, -1), conv_w[i], vec(conv_b[i]),
                          vec(conv_ln_g[i]), vec(conv_ln_b[i]))
        wo = w_out[i].astype(BF16)
        x1, x1b, x1bt = _out_proj(att.reshape(t, -1), cnv.reshape(t, -1), xc, wo[:ATTN_WIDTH],
                            wo[ATTN_WIDTH:], vec(ln1_g[i]), vec(ln1_b[i]), ROW_TILE)
        keys = peer_keys[i].reshape(N_GROUPS, N_KEYS, PEER_HALF).astype(BF16)
        e1, c1, e2, r2 = _peer_routing(x1b, peer_query[i].astype(BF16), keys, ROUTE_TILE)
        peer = _peer_dense(x1bt, peer_u[i].astype(BF16), peer_v[i].astype(BF16).T, e1, c1, e2, r2,
                           DENSE_TOKENS, DENSE_EXPERTS)
        xc = _final(x1, peer, p[i].reshape(t, -1), ple_gate[i].astype(BF16),
                    ple_proj[i].astype(BF16), vec(ln2_g[i]), vec(ln2_b[i]), ROW_TILE)
    return xc.reshape(b, s, d)
```

```python
import jax
import jax.numpy as jnp
from jax import lax
from jax.experimental import pallas as pl
from jax.experimental.pallas import tpu as pltpu

F32 = jnp.float32
BF16 = jnp.bfloat16

D_MODEL = 1024
N_Q_HEADS = 8
N_KV_HEADS = 2
HEAD_DIM = 64
ATTN_WIDTH = N_Q_HEADS * HEAD_DIM
KV_WIDTH = N_KV_HEADS * HEAD_DIM
WINDOW = 128
BLOCK = 128
ROPE_THETA = 500000.0
ROPE_DIM = HEAD_DIM // 4
ROPE_HALF = ROPE_DIM // 2
CONV_WIDTH = D_MODEL - ATTN_WIDTH
CONV_TAPS = 31
PEER_HEADS = 8
N_KEYS = 128
N_EXPERTS = N_KEYS * N_KEYS
PEER_HALF = 128
PEER_TOPK = 16
PLE_DIM = 256
DEPTH = 1
DEEPNORM_ALPHA = (2 * DEPTH) ** 0.25
LN_EPS = 1e-5
NEG_INF = -1e30

LANES = 128
SUBLANES = 8
VMEM_LIMIT_BYTES = 56 * 1024 * 1024

RANK_CODE_BASE = -(2.0 ** 127)
RANK_CODE_STEP = 2.0 ** 120
RANK_CODE_LIMIT = -1.5 * 2.0 ** 126
PAD_SCORE = -(2.0 ** 126)


def _compiler_params(semantics, flags=None):
    return pltpu.CompilerParams(dimension_semantics=semantics,
                                vmem_limit_bytes=VMEM_LIMIT_BYTES, flags=flags)


def _layer_norm(v, g, b):
    mu = jnp.mean(v, axis=-1, keepdims=True)
    d = v - mu
    var = jnp.mean(d * d, axis=-1, keepdims=True)
    return d * lax.rsqrt(var + LN_EPS) * g + b


QK_GROUPS = (ATTN_WIDTH + 2 * KV_WIDTH) // LANES


def _inproj_kernel(x_ref, w_ref, pos_ref, invf_ref, q_ref, k_ref, v_ref, glu_ref):
    h = jnp.dot(x_ref[...].astype(BF16), w_ref[...], preferred_element_type=F32)
    tm = h.shape[0]
    ang = pos_ref[...] * invf_ref[...]
    lane = lax.broadcasted_iota(jnp.int32, (tm, LANES), 1) % HEAD_DIM
    first = lane < ROPE_HALF
    second = jnp.logical_and(lane >= ROPE_HALF, lane < ROPE_DIM)
    cos = jnp.cos(ang)
    sin = jnp.sin(ang)
    sin_signed = jnp.where(first, -sin, jnp.where(second, sin, 0.0))

    def rotate(t):
        up = pltpu.roll(t, LANES - ROPE_HALF, 1)
        dn = pltpu.roll(t, ROPE_HALF, 1)
        return t * cos + jnp.where(first, up, dn) * sin_signed

    scale = HEAD_DIM ** -0.5
    for g in range(ATTN_WIDTH // LANES):
        q_ref[:, g * LANES:(g + 1) * LANES] = (
            rotate(h[:, g * LANES:(g + 1) * LANES]) * scale).astype(BF16)
    o = ATTN_WIDTH
    for g in range(2 * KV_WIDTH // LANES):
        k_ref[:, g * LANES:(g + 1) * LANES] = rotate(
            h[:, o + g * LANES:o + (g + 1) * LANES]).astype(BF16)
    o += 2 * KV_WIDTH
    v_ref[...] = h[:, o:o + 2 * KV_WIDTH].astype(BF16)
    o += 2 * KV_WIDTH
    glu_ref[...] = h[:, o:o + CONV_WIDTH] * jax.nn.sigmoid(
        h[:, o + CONV_WIDTH:o + 2 * CONV_WIDTH])


def _in_proj(x2, w_cat, pos, invf, tm):
    t = x2.shape[0]
    n_in = w_cat.shape[1]
    row = lambda i: (i, 0)
    fixed = lambda i: (0, 0)
    return pl.pallas_call(
        _inproj_kernel,
        grid=(t // tm,),
        in_specs=[pl.BlockSpec((tm, D_MODEL), row),
                  pl.BlockSpec((D_MODEL, n_in), fixed),
                  pl.BlockSpec((tm, 1), row),
                  pl.BlockSpec((1, LANES), fixed)],
        out_specs=[pl.BlockSpec((tm, ATTN_WIDTH), row),
                   pl.BlockSpec((tm, 2 * KV_WIDTH), row),
                   pl.BlockSpec((tm, 2 * KV_WIDTH), row),
                   pl.BlockSpec((tm, CONV_WIDTH), row)],
        out_shape=[jax.ShapeDtypeStruct((t, ATTN_WIDTH), BF16),
                   jax.ShapeDtypeStruct((t, 2 * KV_WIDTH), BF16),
                   jax.ShapeDtypeStruct((t, 2 * KV_WIDTH), BF16),
                   jax.ShapeDtypeStruct((t, CONV_WIDTH), F32)],
        compiler_params=_compiler_params(("parallel",)),
        name="in_proj",
    )(x2, w_cat, pos, invf)


def _attn_kernel(sink_ref, q_ref, k_ref, v_ref, o_ref):
    s_len = q_ref.shape[1]
    nb = s_len // BLOCK
    qi = lax.broadcasted_iota(jnp.int32, (BLOCK, 2 * BLOCK), 0)
    si = lax.broadcasted_iota(jnp.int32, (BLOCK, 2 * BLOCK), 1)
    band = jnp.logical_and(si > qi, si <= qi + WINDOW)
    lane = lax.broadcasted_iota(jnp.int32, (2 * BLOCK, LANES), 1)
    low = lane < HEAD_DIM
    heads_per_group = N_Q_HEADS // N_KV_HEADS
    zero = jnp.zeros((), BF16)

    def body(n, carry):
        start = pl.multiple_of(n * BLOCK, BLOCK)
        prev = pl.multiple_of(jnp.maximum(n - 1, 0) * BLOCK, BLOCK)
        valid = jnp.logical_and(band, si >= jnp.where(n == 0, BLOCK, 0))
        q = q_ref[0, pl.ds(start, BLOCK), :]
        k2 = jnp.concatenate([k_ref[0, pl.ds(prev, BLOCK), :],
                              k_ref[0, pl.ds(start, BLOCK), :]], axis=0)
        v2 = jnp.concatenate([v_ref[0, pl.ds(prev, BLOCK), :],
                              v_ref[0, pl.ds(start, BLOCK), :]], axis=0)
        for pair in range(N_Q_HEADS // 2):
            g = (2 * pair) // heads_per_group
            qp = q[:, pair * LANES:(pair + 1) * LANES]
            kg = k2[:, g * LANES:(g + 1) * LANES]
            vg = v2[:, g * LANES:(g + 1) * LANES]
            out = jnp.zeros((BLOCK, LANES), F32)
            for half in range(2):
                keep = low if half == 0 else jnp.logical_not(low)
                kx = jnp.where(keep, kg, zero)
                vx = jnp.where(keep, vg, zero)
                sink = sink_ref[2 * pair + half]
                s = lax.dot_general(qp, kx, (((1,), (1,)), ((), ())),
                                    preferred_element_type=F32)
                s = jnp.where(valid, s, NEG_INF)
                m = jnp.maximum(jnp.max(s, axis=-1, keepdims=True), sink)
                e = jnp.exp(s - m)
                denom = jnp.sum(e, axis=-1, keepdims=True) + jnp.exp(sink - m)
                pv = jnp.dot(e.astype(BF16), vx, preferred_element_type=F32)
                out = out + pv * (1.0 / denom)
            o_ref[0, pl.ds(start, BLOCK), pair * LANES:(pair + 1) * LANES] = out.astype(BF16)
        return carry

    lax.fori_loop(0, nb, body, 0)


def _attention(sinks, q3, k3, v3):
    b, s, _ = q3.shape
    blk = lambda w: pl.BlockSpec((1, s, w), lambda i: (i, 0, 0))
    return pl.pallas_call(
        _attn_kernel,
        grid=(b,),
        in_specs=[pl.BlockSpec(memory_space=pltpu.SMEM),
                  blk(ATTN_WIDTH), blk(2 * KV_WIDTH), blk(2 * KV_WIDTH)],
        out_specs=blk(ATTN_WIDTH),
        out_shape=jax.ShapeDtypeStruct((b, s, ATTN_WIDTH), BF16),
        compiler_params=_compiler_params(("parallel",)),
        name="attention",
    )(sinks, q3, k3, v3)


CONV_PAD = 32
CONV_ROWS = 128
CONV_COPY_ROWS = 256


def _conv_kernel(h_ref, w_ref, b_ref, g_ref, beta_ref, o_ref, sh_ref, y_ref):
    s_len = h_ref.shape[1]
    padded = CONV_PAD + s_len
    for l in range(CONV_WIDTH // LANES):
        cols = slice(l * LANES, (l + 1) * LANES)
        sh_ref[0, 0:CONV_PAD, :] = jnp.zeros((CONV_PAD, LANES), F32)
        sh_ref[0, CONV_PAD:, :] = h_ref[0, :, cols]
        for r in range(1, SUBLANES):
            for lo in range(SUBLANES, padded, CONV_COPY_ROWS):
                hi = min(lo + CONV_COPY_ROWS, padded)
                sh_ref[r, lo:hi, :] = sh_ref[0, lo - r:hi - r, :]

        def body(c, carry):
            base = pl.multiple_of(CONV_PAD + c * CONV_ROWS, SUBLANES)
            acc = jnp.zeros((CONV_ROWS, LANES), F32) + b_ref[:, cols]
            for j in range(CONV_TAPS):
                a, r = divmod(CONV_TAPS - 1 - j, SUBLANES)
                acc = acc + sh_ref[r, pl.ds(base - SUBLANES * a, CONV_ROWS), :] * w_ref[j:j + 1, cols]
            y_ref[pl.ds(pl.multiple_of(c * CONV_ROWS, SUBLANES), CONV_ROWS), cols] = acc
            return carry

        lax.fori_loop(0, s_len // CONV_ROWS, body, 0)

    def norm(c, carry):
        rows = pl.ds(pl.multiple_of(c * CONV_ROWS, SUBLANES), CONV_ROWS)
        y = _layer_norm(y_ref[rows, :], g_ref[...], beta_ref[...])
        o_ref[0, rows, :] = (y * jax.nn.sigmoid(y)).astype(BF16)
        return carry

    lax.fori_loop(0, s_len // CONV_ROWS, norm, 0)


def _conv_group(h3, conv_w, conv_b, ln_g, ln_b):
    b, s, _ = h3.shape
    blk = pl.BlockSpec((1, s, CONV_WIDTH), lambda i: (i, 0, 0))
    vec = pl.BlockSpec((1, CONV_WIDTH), lambda i: (0, 0))
    return pl.pallas_call(
        _conv_kernel,
        grid=(b,),
        in_specs=[blk, pl.BlockSpec((CONV_TAPS, CONV_WIDTH), lambda i: (0, 0)), vec, vec, vec],
        out_specs=blk,
        out_shape=jax.ShapeDtypeStruct((b, s, CONV_WIDTH), BF16),
        scratch_shapes=[pltpu.VMEM((SUBLANES, CONV_PAD + s, LANES), F32),
                        pltpu.VMEM((s, CONV_WIDTH), F32)],
        compiler_params=_compiler_params(("parallel",)),
        name="conv_group",
    )(h3, conv_w, conv_b, ln_g, ln_b)


def _outproj_kernel(att_ref, cnv_ref, x_ref, wa_ref, wc_ref, g_ref, b_ref, y_ref, yb_ref, ybt_ref):
    mixed = jnp.dot(att_ref[...], wa_ref[...], preferred_element_type=F32)
    mixed = mixed + jnp.dot(cnv_ref[...], wc_ref[...], preferred_element_type=F32)
    y = _layer_norm(DEEPNORM_ALPHA * x_ref[...] + mixed, g_ref[...], b_ref[...])
    y_ref[...] = y
    yb_ref[...] = y.astype(BF16)
    ybt_ref[...] = y.T.astype(BF16)


def _out_proj(att, cnv, x2, wa, wc, g, b, tm):
    t = x2.shape[0]
    row = lambda i: (i, 0)
    fixed = lambda i: (0, 0)
    return pl.pallas_call(
        _outproj_kernel,
        grid=(t // tm,),
        in_specs=[pl.BlockSpec((tm, ATTN_WIDTH), row), pl.BlockSpec((tm, CONV_WIDTH), row),
                  pl.BlockSpec((tm, D_MODEL), row),
                  pl.BlockSpec((ATTN_WIDTH, D_MODEL), fixed),
                  pl.BlockSpec((CONV_WIDTH, D_MODEL), fixed),
                  pl.BlockSpec((1, D_MODEL), fixed), pl.BlockSpec((1, D_MODEL), fixed)],
        out_specs=[pl.BlockSpec((tm, D_MODEL), row), pl.BlockSpec((tm, D_MODEL), row),
                   pl.BlockSpec((D_MODEL, tm), lambda i: (0, i))],
        out_shape=[jax.ShapeDtypeStruct((t, D_MODEL), F32),
                   jax.ShapeDtypeStruct((t, D_MODEL), BF16),
                   jax.ShapeDtypeStruct((D_MODEL, t), BF16)],
        compiler_params=_compiler_params(("parallel",)),
        name="out_proj",
    )(att, cnv, x2, wa, wc, g, b)


N_GROUPS = 2 * PEER_HEADS
KEY_VREGS = N_KEYS // SUBLANES


def _tree_argmax(vals, bases):
    items = [(v, float(b)) for v, b in zip(vals, bases)]
    while len(items) > 1:
        nxt = []
        for i in range(0, len(items) - 1, 2):
            (va, ba), (vb, bb) = items[i], items[i + 1]
            better = vb > va
            nxt.append((jnp.where(better, vb, va), jnp.where(better, bb, ba)))
        if len(items) % 2:
            nxt.append(items[-1])
        items = nxt
    return items[0]


def _tree_max(vals):
    items = list(vals)
    while len(items) > 1:
        nxt = [jnp.maximum(items[i], items[i + 1]) for i in range(0, len(items) - 1, 2)]
        if len(items) % 2:
            nxt.append(items[-1])
        items = nxt
    return items[0]


def _extract_topk(vals, bases, subf, rounds, break_ties):
    vals = list(vals)
    tops = []
    for a in range(rounds):
        code = RANK_CODE_BASE - a * RANK_CODE_STEP
        if break_ties:
            best, best_base = _tree_argmax(vals, bases)
            top = jnp.max(best, axis=0, keepdims=True)
            order = jnp.where(best == top, best_base + subf, 1.0e9)
            winner = jnp.min(order, axis=0, keepdims=True)
            winner_base = winner - subf
            vals = [jnp.where(winner_base == float(bases[k]), code, vals[k])
                    for k in range(len(vals))]
        else:
            top = jnp.max(_tree_max(vals), axis=0, keepdims=True)
            vals = [jnp.where(v == top, code, v) for v in vals]
        tops.append(top)
    return vals, tops


def _tie_flag(marked, rounds):
    count = jnp.zeros(marked[0].shape, F32)
    for m in marked:
        count = count + jnp.where(m < RANK_CODE_LIMIT, 1.0, 0.0)
    return jnp.where(jnp.sum(count, axis=0, keepdims=True) == float(rounds), 0.0, 1.0)


PAIR_VREGS = ((0, 0, 8, 0), (0, 8, 8, 8), (1, 0, 8, 16), (2, 0, 5, 32), (3, 0, 4, 48),
              (4, 0, 3, 64), (5, 0, 2, 80), (6, 0, 2, 96), (7, 0, 2, 112))
PAIR_TAIL_BASE = 128


def _routing_kernel(x_ref, wq_ref, keys_ref, e1_ref, c1_ref, e2_ref, r2_ref,
                    qb_ref, sc_ref, mk_ref, top_ref):
    tt = x_ref.shape[0]
    n_chunks = tt // LANES
    qb_ref[...] = jnp.dot(x_ref[...], wq_ref[...], preferred_element_type=F32).astype(BF16)
    for g in range(N_GROUPS):
        sc_ref[g] = lax.dot_general(keys_ref[g], qb_ref[:, g * PEER_HALF:(g + 1) * PEER_HALF],
                                    (((1,), (1,)), ((), ())), preferred_element_type=F32)

    sub = lax.broadcasted_iota(jnp.int32, (SUBLANES, LANES), 0)
    subf = sub.astype(F32)
    key_bases = tuple(SUBLANES * k for k in range(KEY_VREGS))

    pair_bases = tuple(v[3] for v in PAIR_VREGS) + (PAIR_TAIL_BASE,)
    no_ties = jnp.zeros((1, LANES), F32)

    def level1(g, tie, break_ties):
        for c in range(n_chunks):
            cols = slice(c * LANES, (c + 1) * LANES)
            vals = [sc_ref[g, SUBLANES * k:SUBLANES * (k + 1), cols] for k in range(KEY_VREGS)]
            marked, tops = _extract_topk(vals, key_bases, subf, PEER_TOPK, break_ties)
            if not break_ties:
                tie = jnp.maximum(tie, _tie_flag(marked, PEER_TOPK))
            for k in range(KEY_VREGS):
                mk_ref[g, SUBLANES * k:SUBLANES * (k + 1), cols] = marked[k]
            for a in range(PEER_TOPK):
                top_ref[g, a:a + 1, cols] = tops[a]
        return tie

    def level2(h, tie, break_ties):
        g1 = 2 * h
        g2 = 2 * h + 1
        for c in range(n_chunks):
            cols = slice(c * LANES, (c + 1) * LANES)
            v2 = (top_ref[g2, 0:SUBLANES, cols], top_ref[g2, SUBLANES:2 * SUBLANES, cols])
            m1 = top_ref[g1, 0:1, cols]
            m2 = top_ref[g2, 0:1, cols]
            cands = []
            for a, b0, used, _ in PAIR_VREGS:
                cand = top_ref[g1, a:a + 1, cols] + v2[b0 // SUBLANES]
                cands.append(cand if used == SUBLANES else jnp.where(sub < used, cand, PAD_SCORE))
            cands.append(top_ref[g1, SUBLANES:2 * SUBLANES, cols] + m2)
            marked, _ = _extract_topk(cands, pair_bases, subf, PEER_TOPK, break_ties)
            if not break_ties:
                tie = jnp.maximum(tie, _tie_flag(marked, PEER_TOPK))
            best = m1 + m2
            picked = [m < RANK_CODE_LIMIT for m in marked]
            ones = [jnp.where(p, 1.0, 0.0) for p in picked]
            zsum = jnp.zeros((SUBLANES, LANES), F32)
            for p, cand in zip(picked, cands):
                zsum = zsum + jnp.where(p, jnp.exp(cand - best), 0.0)
            inv_z = 0.5 / jnp.sum(zsum, axis=0, keepdims=True)
            widths = [jnp.sum(ones[0] + ones[1], axis=0, keepdims=True)]
            widths += [jnp.sum(ones[j], axis=0, keepdims=True) for j in range(2, len(PAIR_VREGS))]
            widths += [ones[-1][s:s + 1, :] for s in range(SUBLANES)]
            widths = [jnp.broadcast_to(w, (SUBLANES, LANES)) for w in widths]
            for kk in range(KEY_VREGS // 2):
                pair_rows = slice(2 * SUBLANES * kk, 2 * SUBLANES * (kk + 1))
                e2_pair, r2_pair = [], []
                for k in (2 * kk, 2 * kk + 1):
                    rows = slice(SUBLANES * k, SUBLANES * (k + 1))
                    mk1 = mk_ref[g1, rows, cols]
                    cnt1 = jnp.zeros((SUBLANES, LANES), F32)
                    for a in range(PEER_TOPK):
                        cnt1 = jnp.where(mk1 == RANK_CODE_BASE - a * RANK_CODE_STEP, widths[a], cnt1)
                    c1_ref[h, rows, cols] = cnt1
                    e1_ref[h, rows, cols] = jnp.exp(sc_ref[g1, rows, cols] - m1) * inv_z
                    mk2 = mk_ref[g2, rows, cols]
                    rank2 = jnp.floor((RANK_CODE_BASE - mk2) * (1.0 / RANK_CODE_STEP) + 0.5)
                    r2_pair.append(jnp.where(mk2 < RANK_CODE_LIMIT, rank2, float(PEER_TOPK)))
                    e2_pair.append(jnp.exp(sc_ref[g2, rows, cols] - m2))
                r2_ref[h, pair_rows, cols] = jnp.concatenate(r2_pair, axis=0).astype(BF16)
                e2_ref[h, pair_rows, cols] = jnp.concatenate(e2_pair, axis=0).astype(BF16)
        return tie

    tie = lax.fori_loop(0, N_GROUPS, lambda g, t: level1(g, t, False), no_ties)
    tie = lax.fori_loop(0, PEER_HEADS, lambda h, t: level2(h, t, False), tie)

    @pl.when(jnp.max(tie) > 0.0)
    def _():
        lax.fori_loop(0, N_GROUPS, lambda g, t: level1(g, t, True), no_ties)
        lax.fori_loop(0, PEER_HEADS, lambda h, t: level2(h, t, True), no_ties)


def _peer_routing(x1b, wq, keys, tt):
    t = x1b.shape[0]
    out_blk = pl.BlockSpec((PEER_HEADS, N_KEYS, tt), lambda i: (0, 0, i))
    sds = lambda dt: jax.ShapeDtypeStruct((PEER_HEADS, N_KEYS, t), dt)
    return pl.pallas_call(
        _routing_kernel,
        grid=(t // tt,),
        in_specs=[pl.BlockSpec((tt, D_MODEL), lambda i: (i, 0)),
                  pl.BlockSpec((D_MODEL, N_GROUPS * PEER_HALF), lambda i: (0, 0)),
                  pl.BlockSpec((N_GROUPS, N_KEYS, PEER_HALF), lambda i: (0, 0, 0))],
        out_specs=[out_blk] * 4,
        out_shape=[sds(F32), sds(F32), sds(BF16), sds(BF16)],
        scratch_shapes=[pltpu.VMEM((tt, N_GROUPS * PEER_HALF), BF16),
                        pltpu.VMEM((N_GROUPS, N_KEYS, tt), F32),
                        pltpu.VMEM((N_GROUPS, N_KEYS, tt), F32),
                        pltpu.VMEM((N_GROUPS, PEER_TOPK, tt), F32)],
        compiler_params=_compiler_params(("parallel",)),
        name="peer_routing",
    )(x1b, wq, keys)


DENSE_SUB = 512
BF16_ROWS = 2 * SUBLANES
DENSE_FLAGS = None


def _dense_kernel(xt_ref, u_ref, vt_ref, e1_ref, c1_ref, e2_in_ref, r2_in_ref, o_ref,
                  acc_ref, ht_ref, act_ref, e2_ref, r2_ref):
    j = pl.program_id(1)
    last = pl.num_programs(1) - 1
    _, eb, tt = ht_ref.shape
    n_sub = eb // DENSE_SUB
    zero = jnp.zeros((), BF16)
    write_slot = j % 2
    read_slot = 1 - write_slot

    def sub_rows(sb):
        return slice(sb * DENSE_SUB, (sb + 1) * DENSE_SUB)

    def stage_a(sb):
        ht_ref[write_slot, sub_rows(sb), :] = jnp.dot(u_ref[sub_rows(sb), :], xt_ref[...],
                                                      preferred_element_type=F32)

    def stage_b(sb):
        for il in range(sb * DENSE_SUB // N_KEYS, (sb + 1) * DENSE_SUB // N_KEYS):
            for tc in range(tt // LANES):
                cols = slice(tc * LANES, (tc + 1) * LANES)
                c1 = [jnp.broadcast_to(c1_ref[h, il:il + 1, cols], (BF16_ROWS, LANES)).astype(BF16)
                      for h in range(PEER_HEADS)]
                e1 = [jnp.broadcast_to(e1_ref[h, il:il + 1, cols], (BF16_ROWS, LANES)).astype(BF16)
                      for h in range(PEER_HEADS)]
                for p in range(N_KEYS // BF16_ROWS):
                    keys = slice(p * BF16_ROWS, (p + 1) * BF16_ROWS)
                    rows = slice(il * N_KEYS + p * BF16_ROWS, il * N_KEYS + (p + 1) * BF16_ROWS)
                    gate = jnp.zeros((BF16_ROWS, LANES), BF16)
                    for h in range(PEER_HEADS):
                        picked = jnp.where(r2_ref[h, keys, cols] < c1[h], e2_ref[h, keys, cols], zero)
                        gate = gate + picked * e1[h]
                    hh = ht_ref[read_slot, rows, cols]
                    gelu2 = hh * (1.0 + lax.erf(hh * (2.0 ** -0.5)))
                    act_ref[rows, cols] = gelu2.astype(BF16) * gate
        acc_ref[...] += jnp.dot(vt_ref[0, :, sub_rows(sb)], act_ref[sub_rows(sb), :],
                                preferred_element_type=F32)

    @pl.when(j == 0)
    def _():
        acc_ref[...] = jnp.zeros_like(acc_ref)
        e2_ref[...] = e2_in_ref[...]
        r2_ref[...] = r2_in_ref[...]
        for sb in range(n_sub):
            stage_a(sb)

    @pl.when(jnp.logical_and(j > 0, j < last))
    def _():
        stage_a(0)
        for sb in range(n_sub):
            if sb + 1 < n_sub:
                stage_a(sb + 1)
            stage_b(sb)

    @pl.when(j == last)
    def _():
        for sb in range(n_sub):
            stage_b(sb)
        o_ref[...] = acc_ref[...].T


def _peer_dense(x1bt, u, vt, e1, c1, e2, r2, tt, eb):
    t = x1bt.shape[1]
    n_blocks = N_EXPERTS // eb
    route = pl.BlockSpec((PEER_HEADS, N_KEYS, tt), lambda i, j: (0, 0, i))
    route_rows = pl.BlockSpec((PEER_HEADS, eb // N_KEYS, tt),
                              lambda i, j: (0, jnp.maximum(j - 1, 0), i))
    return pl.pallas_call(
        _dense_kernel,
        grid=(t // tt, n_blocks + 1),
        in_specs=[pl.BlockSpec((D_MODEL, tt), lambda i, j: (0, i)),
                  pl.BlockSpec((eb, D_MODEL), lambda i, j: (jnp.minimum(j, n_blocks - 1), 0)),
                  pl.BlockSpec((1, D_MODEL, eb), lambda i, j: (jnp.maximum(j - 1, 0), 0, 0)),
                  route_rows, route_rows, route, route],
        out_specs=pl.BlockSpec((tt, D_MODEL), lambda i, j: (i, 0)),
        out_shape=jax.ShapeDtypeStruct((t, D_MODEL), F32),
        scratch_shapes=[pltpu.VMEM((D_MODEL, tt), F32),
                        pltpu.VMEM((2, eb, tt), F32),
                        pltpu.VMEM((eb, tt), BF16),
                        pltpu.VMEM((PEER_HEADS, N_KEYS, tt), BF16),
                        pltpu.VMEM((PEER_HEADS, N_KEYS, tt), BF16)],
        compiler_params=_compiler_params(("parallel", "arbitrary"), DENSE_FLAGS),
        name="peer_dense",
    )(x1bt, u, vt, e1, c1, e2, r2)


def _final_kernel(x1_ref, peer_ref, p_ref, wg_ref, wp_ref, g_ref, b_ref, o_ref):
    r = DEEPNORM_ALPHA * x1_ref[...] + peer_ref[...]
    gate = jax.nn.sigmoid(jnp.dot(r.astype(BF16), wg_ref[...], preferred_element_type=F32))
    emb = jnp.dot(p_ref[...].astype(BF16), wp_ref[...], preferred_element_type=F32)
    o_ref[...] = _layer_norm(r + gate * emb, g_ref[...], b_ref[...])


def _final(x1, peer, p2, wg, wp, g, b, tm):
    t = x1.shape[0]
    row = lambda i: (i, 0)
    fixed = lambda i: (0, 0)
    return pl.pallas_call(
        _final_kernel,
        grid=(t // tm,),
        in_specs=[pl.BlockSpec((tm, D_MODEL), row), pl.BlockSpec((tm, D_MODEL), row),
                  pl.BlockSpec((tm, PLE_DIM), row),
                  pl.BlockSpec((D_MODEL, D_MODEL), fixed), pl.BlockSpec((PLE_DIM, D_MODEL), fixed),
                  pl.BlockSpec((1, D_MODEL), fixed), pl.BlockSpec((1, D_MODEL), fixed)],
        out_specs=pl.BlockSpec((tm, D_MODEL), row),
        out_shape=jax.ShapeDtypeStruct((t, D_MODEL), F32),
        compiler_params=_compiler_params(("parallel",)),
        name="final",
    )(x1, peer, p2, wg, wp, g, b)


ROW_TILE = 512
ROUTE_TILE = 256
DENSE_TOKENS = 1024
DENSE_EXPERTS = 1024


def _rope_lane_table():
    inv_freq = ROPE_THETA ** (-jnp.arange(ROPE_HALF, dtype=F32) * (2.0 / ROPE_DIM))
    lane = jnp.arange(LANES) % HEAD_DIM
    return jnp.where(lane < ROPE_DIM, inv_freq[lane % ROPE_HALF], 0.0).reshape(1, LANES).astype(F32)


def _dup_kv(w):
    h0, h1 = w[:, :HEAD_DIM], w[:, HEAD_DIM:]
    return jnp.concatenate([h0, h0, h1, h1], axis=1)


def kernel(x, p, positions, w_in, sinks, conv_w, conv_b, conv_ln_g, conv_ln_b, w_out, ln1_g, ln1_b,
           peer_query, peer_keys, peer_u, peer_v, ple_proj, ple_gate, ln2_g, ln2_b):
    b, s, d = x.shape
    t = b * s
    pos = positions.reshape(t, 1).astype(F32)
    invf = _rope_lane_table()
    xc = x.reshape(t, d)
    vec = lambda a: a.reshape(1, -1)
    for i in range(DEPTH):
        w = w_in[i]
        o = ATTN_WIDTH
        w_cat = jnp.concatenate(
            [w[:, :o], _dup_kv(w[:, o:o + KV_WIDTH]), _dup_kv(w[:, o + KV_WIDTH:o + 2 * KV_WIDTH]),
             w[:, o + 2 * KV_WIDTH:]], axis=1).astype(BF16)
        q, k, v, glu = _in_proj(xc, w_cat, pos, invf, ROW_TILE)
        att = _attention(sinks[i], q.reshape(b, s, -1), k.reshape(b, s, -1), v.reshape(b, s, -1))
        cnv = _conv_group(glu.reshape(b, s, -1), conv_w[i], vec(conv_b[i]),
                          vec(conv_ln_g[i]), vec(conv_ln_b[i]))
        wo = w_out[i].astype(BF16)
        x1, x1b, x1bt = _out_proj(att.reshape(t, -1), cnv.reshape(t, -1), xc, wo[:ATTN_WIDTH],
                            wo[ATTN_WIDTH:], vec(ln1_g[i]), vec(ln1_b[i]), ROW_TILE)
        keys = peer_keys[i].reshape(N_GROUPS, N_KEYS, PEER_HALF).astype(BF16)
        e1, c1, e2, r2 = _peer_routing(x1b, peer_query[i].astype(BF16), keys, ROUTE_TILE)
        vt = peer_v[i].astype(BF16).reshape(N_EXPERTS // DENSE_EXPERTS, DENSE_EXPERTS, d).transpose(0, 2, 1)
        peer = _peer_dense(x1bt, peer_u[i].astype(BF16), vt, e1, c1, e2, r2,
                           DENSE_TOKENS, DENSE_EXPERTS)
        xc = _final(x1, peer, p[i].reshape(t, -1), ple_gate[i].astype(BF16),
                    ple_proj[i].astype(BF16), vec(ln2_g[i]), vec(ln2_b[i]), ROW_TILE)
    return xc.reshape(b, s, d)
```

```python
import jax
import jax.numpy as jnp
from jax import lax
from jax.experimental import pallas as pl
from jax.experimental.pallas import tpu as pltpu

F32 = jnp.float32
BF16 = jnp.bfloat16

D_MODEL = 1024
N_Q_HEADS = 8
N_KV_HEADS = 2
HEAD_DIM = 64
ATTN_WIDTH = N_Q_HEADS * HEAD_DIM
KV_WIDTH = N_KV_HEADS * HEAD_DIM
WINDOW = 128
BLOCK = 128
ROPE_THETA = 500000.0
ROPE_DIM = HEAD_DIM // 4
ROPE_HALF = ROPE_DIM // 2
CONV_WIDTH = D_MODEL - ATTN_WIDTH
CONV_TAPS = 31
PEER_HEADS = 8
N_KEYS = 128
N_EXPERTS = N_KEYS * N_KEYS
PEER_HALF = 128
PEER_TOPK = 16
PLE_DIM = 256
DEPTH = 1
DEEPNORM_ALPHA = (2 * DEPTH) ** 0.25
LN_EPS = 1e-5
NEG_INF = -1e30

LANES = 128
SUBLANES = 8
VMEM_LIMIT_BYTES = 56 * 1024 * 1024

RANK_CODE_BASE = -(2.0 ** 127)
RANK_CODE_STEP = 2.0 ** 120
RANK_CODE_LIMIT = -1.5 * 2.0 ** 126
PAD_SCORE = -(2.0 ** 126)


def _compiler_params(semantics, flags=None):
    return pltpu.CompilerParams(dimension_semantics=semantics,
                                vmem_limit_bytes=VMEM_LIMIT_BYTES, flags=flags)


def _layer_norm(v, g, b):
    mu = jnp.mean(v, axis=-1, keepdims=True)
    d = v - mu
    var = jnp.mean(d * d, axis=-1, keepdims=True)
    return d * lax.rsqrt(var + LN_EPS) * g + b


QK_GROUPS = (ATTN_WIDTH + 2 * KV_WIDTH) // LANES


def _inproj_kernel(x_ref, w_ref, pos_ref, invf_ref, q_ref, k_ref, v_ref, glu_ref):
    h = jnp.dot(x_ref[...].astype(BF16), w_ref[...], preferred_element_type=F32)
    tm = h.shape[0]
    ang = pos_ref[...] * invf_ref[...]
    lane = lax.broadcasted_iota(jnp.int32, (tm, LANES), 1) % HEAD_DIM
    first = lane < ROPE_HALF
    second = jnp.logical_and(lane >= ROPE_HALF, lane < ROPE_DIM)
    cos = jnp.cos(ang)
    sin = jnp.sin(ang)
    sin_signed = jnp.where(first, -sin, jnp.where(second, sin, 0.0))

    def rotate(t):
        up = pltpu.roll(t, LANES - ROPE_HALF, 1)
        dn = pltpu.roll(t, ROPE_HALF, 1)
        return t * cos + jnp.where(first, up, dn) * sin_signed

    scale = HEAD_DIM ** -0.5
    for g in range(ATTN_WIDTH // LANES):
        q_ref[:, g * LANES:(g + 1) * LANES] = (
            rotate(h[:, g * LANES:(g + 1) * LANES]) * scale).astype(BF16)
    o = ATTN_WIDTH
    for g in range(2 * KV_WIDTH // LANES):
        k_ref[:, g * LANES:(g + 1) * LANES] = rotate(
            h[:, o + g * LANES:o + (g + 1) * LANES]).astype(BF16)
    o += 2 * KV_WIDTH
    v_ref[...] = h[:, o:o + 2 * KV_WIDTH].astype(BF16)
    o += 2 * KV_WIDTH
    glu_ref[...] = h[:, o:o + CONV_WIDTH] * jax.nn.sigmoid(
        h[:, o + CONV_WIDTH:o + 2 * CONV_WIDTH])


def _in_proj(x2, w_cat, pos, invf, tm):
    t = x2.shape[0]
    n_in = w_cat.shape[1]
    row = lambda i: (i, 0)
    fixed = lambda i: (0, 0)
    return pl.pallas_call(
        _inproj_kernel,
        grid=(t // tm,),
        in_specs=[pl.BlockSpec((tm, D_MODEL), row),
                  pl.BlockSpec((D_MODEL, n_in), fixed),
                  pl.BlockSpec((tm, 1), row),
                  pl.BlockSpec((1, LANES), fixed)],
        out_specs=[pl.BlockSpec((tm, ATTN_WIDTH), row),
                   pl.BlockSpec((tm, 2 * KV_WIDTH), row),
                   pl.BlockSpec((tm, 2 * KV_WIDTH), row),
                   pl.BlockSpec((tm, CONV_WIDTH), row)],
        out_shape=[jax.ShapeDtypeStruct((t, ATTN_WIDTH), BF16),
                   jax.ShapeDtypeStruct((t, 2 * KV_WIDTH), BF16),
                   jax.ShapeDtypeStruct((t, 2 * KV_WIDTH), BF16),
                   jax.ShapeDtypeStruct((t, CONV_WIDTH), F32)],
        compiler_params=_compiler_params(("parallel",)),
        name="in_proj",
    )(x2, w_cat, pos, invf)


def _attn_kernel(sink_ref, q_ref, k_ref, v_ref, o_ref):
    s_len = q_ref.shape[1]
    nb = s_len // BLOCK
    qi = lax.broadcasted_iota(jnp.int32, (BLOCK, 2 * BLOCK), 0)
    si = lax.broadcasted_iota(jnp.int32, (BLOCK, 2 * BLOCK), 1)
    band = jnp.logical_and(si > qi, si <= qi + WINDOW)
    lane = lax.broadcasted_iota(jnp.int32, (2 * BLOCK, LANES), 1)
    low = lane < HEAD_DIM
    heads_per_group = N_Q_HEADS // N_KV_HEADS
    zero = jnp.zeros((), BF16)

    def body(n, carry):
        start = pl.multiple_of(n * BLOCK, BLOCK)
        prev = pl.multiple_of(jnp.maximum(n - 1, 0) * BLOCK, BLOCK)
        valid = jnp.logical_and(band, si >= jnp.where(n == 0, BLOCK, 0))
        q = q_ref[0, pl.ds(start, BLOCK), :]
        k2 = jnp.concatenate([k_ref[0, pl.ds(prev, BLOCK), :],
                              k_ref[0, pl.ds(start, BLOCK), :]], axis=0)
        v2 = jnp.concatenate([v_ref[0, pl.ds(prev, BLOCK), :],
                              v_ref[0, pl.ds(start, BLOCK), :]], axis=0)
        for pair in range(N_Q_HEADS // 2):
            g = (2 * pair) // heads_per_group
            qp = q[:, pair * LANES:(pair + 1) * LANES]
            kg = k2[:, g * LANES:(g + 1) * LANES]
            vg = v2[:, g * LANES:(g + 1) * LANES]
            out = jnp.zeros((BLOCK, LANES), F32)
            for half in range(2):
                keep = low if half == 0 else jnp.logical_not(low)
                kx = jnp.where(keep, kg, zero)
                vx = jnp.where(keep, vg, zero)
                sink = sink_ref[2 * pair + half]
                s = lax.dot_general(qp, kx, (((1,), (1,)), ((), ())),
                                    preferred_element_type=F32)
                s = jnp.where(valid, s, NEG_INF)
                m = jnp.maximum(jnp.max(s, axis=-1, keepdims=True), sink)
                e = jnp.exp(s - m)
                denom = jnp.sum(e, axis=-1, keepdims=True) + jnp.exp(sink - m)
                pv = jnp.dot(e.astype(BF16), vx, preferred_element_type=F32)
                out = out + pv * (1.0 / denom)
            o_ref[0, pl.ds(start, BLOCK), pair * LANES:(pair + 1) * LANES] = out.astype(BF16)
        return carry

    lax.fori_loop(0, nb, body, 0)


def _attention(sinks, q3, k3, v3):
    b, s, _ = q3.shape
    blk = lambda w: pl.BlockSpec((1, s, w), lambda i: (i, 0, 0))
    return pl.pallas_call(
        _attn_kernel,
        grid=(b,),
        in_specs=[pl.BlockSpec(memory_space=pltpu.SMEM),
                  blk(ATTN_WIDTH), blk(2 * KV_WIDTH), blk(2 * KV_WIDTH)],
        out_specs=blk(ATTN_WIDTH),
        out_shape=jax.ShapeDtypeStruct((b, s, ATTN_WIDTH), BF16),
        compiler_params=_compiler_params(("parallel",)),
        name="attention",
    )(sinks, q3, k3, v3)


CONV_PAD = 32
CONV_ROWS = 128
CONV_COPY_ROWS = 256


def _conv_kernel(h_ref, w_ref, b_ref, g_ref, beta_ref, o_ref, sh_ref, y_ref):
    s_len = h_ref.shape[1]
    padded = CONV_PAD + s_len
    for l in range(CONV_WIDTH // LANES):
        cols = slice(l * LANES, (l + 1) * LANES)
        sh_ref[0, 0:CONV_PAD, :] = jnp.zeros((CONV_PAD, LANES), F32)
        sh_ref[0, CONV_PAD:, :] = h_ref[0, :, cols]
        for r in range(1, SUBLANES):
            for lo in range(SUBLANES, padded, CONV_COPY_ROWS):
                hi = min(lo + CONV_COPY_ROWS, padded)
                sh_ref[r, lo:hi, :] = sh_ref[0, lo - r:hi - r, :]

        def body(c, carry):
            base = pl.multiple_of(CONV_PAD + c * CONV_ROWS, SUBLANES)
            acc = jnp.zeros((CONV_ROWS, LANES), F32) + b_ref[:, cols]
            for j in range(CONV_TAPS):
                a, r = divmod(CONV_TAPS - 1 - j, SUBLANES)
                acc = acc + sh_ref[r, pl.ds(base - SUBLANES * a, CONV_ROWS), :] * w_ref[j:j + 1, cols]
            y_ref[pl.ds(pl.multiple_of(c * CONV_ROWS, SUBLANES), CONV_ROWS), cols] = acc
            return carry

        lax.fori_loop(0, s_len // CONV_ROWS, body, 0)

    def norm(c, carry):
        rows = pl.ds(pl.multiple_of(c * CONV_ROWS, SUBLANES), CONV_ROWS)
        y = _layer_norm(y_ref[rows, :], g_ref[...], beta_ref[...])
        o_ref[0, rows, :] = (y * jax.nn.sigmoid(y)).astype(BF16)
        return carry

    lax.fori_loop(0, s_len // CONV_ROWS, norm, 0)


def _conv_group(h3, conv_w, conv_b, ln_g, ln_b):
    b, s, _ = h3.shape
    blk = pl.BlockSpec((1, s, CONV_WIDTH), lambda i: (i, 0, 0))
    vec = pl.BlockSpec((1, CONV_WIDTH), lambda i: (0, 0))
    return pl.pallas_call(
        _conv_kernel,
        grid=(b,),
        in_specs=[blk, pl.BlockSpec((CONV_TAPS, CONV_WIDTH), lambda i: (0, 0)), vec, vec, vec],
        out_specs=blk,
        out_shape=jax.ShapeDtypeStruct((b, s, CONV_WIDTH), BF16),
        scratch_shapes=[pltpu.VMEM((SUBLANES, CONV_PAD + s, LANES), F32),
                        pltpu.VMEM((s, CONV_WIDTH), F32)],
        compiler_params=_compiler_params(("parallel",)),
        name="conv_group",
    )(h3, conv_w, conv_b, ln_g, ln_b)


def _outproj_kernel(att_ref, cnv_ref, x_ref, wa_ref, wc_ref, g_ref, b_ref, y_ref, yb_ref, ybt_ref):
    mixed = jnp.dot(att_ref[...], wa_ref[...], preferred_element_type=F32)
    mixed = mixed + jnp.dot(cnv_ref[...], wc_ref[...], preferred_element_type=F32)
    y = _layer_norm(DEEPNORM_ALPHA * x_ref[...] + mixed, g_ref[...], b_ref[...])
    y_ref[...] = y
    yb_ref[...] = y.astype(BF16)
    ybt_ref[...] = y.T.astype(BF16)


def _out_proj(att, cnv, x2, wa, wc, g, b, tm):
    t = x2.shape[0]
    row = lambda i: (i, 0)
    fixed = lambda i: (0, 0)
    return pl.pallas_call(
        _outproj_kernel,
        grid=(t // tm,),
        in_specs=[pl.BlockSpec((tm, ATTN_WIDTH), row), pl.BlockSpec((tm, CONV_WIDTH), row),
                  pl.BlockSpec((tm, D_MODEL), row),
                  pl.BlockSpec((ATTN_WIDTH, D_MODEL), fixed),
                  pl.BlockSpec((CONV_WIDTH, D_MODEL), fixed),
                  pl.BlockSpec((1, D_MODEL), fixed), pl.BlockSpec((1, D_MODEL), fixed)],
        out_specs=[pl.BlockSpec((tm, D_MODEL), row), pl.BlockSpec((tm, D_MODEL), row),
                   pl.BlockSpec((D_MODEL, tm), lambda i: (0, i))],
        out_shape=[jax.ShapeDtypeStruct((t, D_MODEL), F32),
                   jax.ShapeDtypeStruct((t, D_MODEL), BF16),
                   jax.ShapeDtypeStruct((D_MODEL, t), BF16)],
        compiler_params=_compiler_params(("parallel",)),
        name="out_proj",
    )(att, cnv, x2, wa, wc, g, b)


N_GROUPS = 2 * PEER_HEADS
KEY_VREGS = N_KEYS // SUBLANES


def _tree_argmax(vals, bases):
    items = [(v, float(b)) for v, b in zip(vals, bases)]
    while len(items) > 1:
        nxt = []
        for i in range(0, len(items) - 1, 2):
            (va, ba), (vb, bb) = items[i], items[i + 1]
            better = vb > va
            nxt.append((jnp.where(better, vb, va), jnp.where(better, bb, ba)))
        if len(items) % 2:
            nxt.append(items[-1])
        items = nxt
    return items[0]


def _tree_max(vals):
    items = list(vals)
    while len(items) > 1:
        nxt = [jnp.maximum(items[i], items[i + 1]) for i in range(0, len(items) - 1, 2)]
        if len(items) % 2:
            nxt.append(items[-1])
        items = nxt
    return items[0]


def _extract_topk(vals, bases, subf, rounds, break_ties):
    vals = list(vals)
    tops = []
    for a in range(rounds):
        code = RANK_CODE_BASE - a * RANK_CODE_STEP
        if break_ties:
            best, best_base = _tree_argmax(vals, bases)
            top = jnp.max(best, axis=0, keepdims=True)
            order = jnp.where(best == top, best_base + subf, 1.0e9)
            winner = jnp.min(order, axis=0, keepdims=True)
            winner_base = winner - subf
            vals = [jnp.where(winner_base == float(bases[k]), code, vals[k])
                    for k in range(len(vals))]
        else:
            top = jnp.max(_tree_max(vals), axis=0, keepdims=True)
            vals = [jnp.where(v == top, code, v) for v in vals]
        tops.append(top)
    return vals, tops


def _tie_flag(marked, rounds):
    count = jnp.zeros(marked[0].shape, F32)
    for m in marked:
        count = count + jnp.where(m < RANK_CODE_LIMIT, 1.0, 0.0)
    return jnp.where(jnp.sum(count, axis=0, keepdims=True) == float(rounds), 0.0, 1.0)


PAIR_VREGS = ((0, 0, 8, 0), (0, 8, 8, 8), (1, 0, 8, 16), (2, 0, 5, 32), (3, 0, 4, 48),
              (4, 0, 3, 64), (5, 0, 2, 80), (6, 0, 2, 96), (7, 0, 2, 112))
PAIR_TAIL_BASE = 128


def _routing_kernel(x_ref, wq_ref, keys_ref, e1_ref, c1_ref, e2_ref, r2_ref,
                    qb_ref, sc_ref, mk_ref, top_ref):
    tt = x_ref.shape[0]
    n_chunks = tt // LANES
    qb_ref[...] = jnp.dot(x_ref[...], wq_ref[...], preferred_element_type=F32).astype(BF16)
    for g in range(N_GROUPS):
        sc_ref[g] = lax.dot_general(keys_ref[g], qb_ref[:, g * PEER_HALF:(g + 1) * PEER_HALF],
                                    (((1,), (1,)), ((), ())), preferred_element_type=F32)

    sub = lax.broadcasted_iota(jnp.int32, (SUBLANES, LANES), 0)
    subf = sub.astype(F32)
    key_bases = tuple(SUBLANES * k for k in range(KEY_VREGS))

    pair_bases = tuple(v[3] for v in PAIR_VREGS) + (PAIR_TAIL_BASE,)
    no_ties = jnp.zeros((1, LANES), F32)

    def level1(g, tie, break_ties):
        for c in range(n_chunks):
            cols = slice(c * LANES, (c + 1) * LANES)
            vals = [sc_ref[g, SUBLANES * k:SUBLANES * (k + 1), cols] for k in range(KEY_VREGS)]
            marked, tops = _extract_topk(vals, key_bases, subf, PEER_TOPK, break_ties)
            if not break_ties:
                tie = jnp.maximum(tie, _tie_flag(marked, PEER_TOPK))
            for k in range(KEY_VREGS):
                mk_ref[g, SUBLANES * k:SUBLANES * (k + 1), cols] = marked[k]
            for a in range(PEER_TOPK):
                top_ref[g, a:a + 1, cols] = tops[a]
        return tie

    def level2(h, tie, break_ties):
        g1 = 2 * h
        g2 = 2 * h + 1
        for c in range(n_chunks):
            cols = slice(c * LANES, (c + 1) * LANES)
            v2 = (top_ref[g2, 0:SUBLANES, cols], top_ref[g2, SUBLANES:2 * SUBLANES, cols])
            m1 = top_ref[g1, 0:1, cols]
            m2 = top_ref[g2, 0:1, cols]
            cands = []
            for a, b0, used, _ in PAIR_VREGS:
                cand = top_ref[g1, a:a + 1, cols] + v2[b0 // SUBLANES]
                cands.append(cand if used == SUBLANES else jnp.where(sub < used, cand, PAD_SCORE))
            cands.append(top_ref[g1, SUBLANES:2 * SUBLANES, cols] + m2)
            marked, _ = _extract_topk(cands, pair_bases, subf, PEER_TOPK, break_ties)
            if not break_ties:
                tie = jnp.maximum(tie, _tie_flag(marked, PEER_TOPK))
            best = m1 + m2
            picked = [m < RANK_CODE_LIMIT for m in marked]
            ones = [jnp.where(p, 1.0, 0.0) for p in picked]
            zsum = jnp.zeros((SUBLANES, LANES), F32)
            for p, cand in zip(picked, cands):
                zsum = zsum + jnp.where(p, jnp.exp(cand - best), 0.0)
            inv_z = 0.5 / jnp.sum(zsum, axis=0, keepdims=True)
            widths = [jnp.sum(ones[0] + ones[1], axis=0, keepdims=True)]
            widths += [jnp.sum(ones[j], axis=0, keepdims=True) for j in range(2, len(PAIR_VREGS))]
            widths += [ones[-1][s:s + 1, :] for s in range(SUBLANES)]
            widths = [jnp.broadcast_to(w, (SUBLANES, LANES)) for w in widths]
            for kk in range(KEY_VREGS // 2):
                pair_rows = slice(2 * SUBLANES * kk, 2 * SUBLANES * (kk + 1))
                e2_pair, r2_pair = [], []
                for k in (2 * kk, 2 * kk + 1):
                    rows = slice(SUBLANES * k, SUBLANES * (k + 1))
                    mk1 = mk_ref[g1, rows, cols]
                    cnt1 = jnp.zeros((SUBLANES, LANES), F32)
                    for a in range(PEER_TOPK):
                        cnt1 = jnp.where(mk1 == RANK_CODE_BASE - a * RANK_CODE_STEP, widths[a], cnt1)
                    c1_ref[h, rows, cols] = cnt1
                    e1_ref[h, rows, cols] = jnp.exp(sc_ref[g1, rows, cols] - m1) * inv_z
                    mk2 = mk_ref[g2, rows, cols]
                    rank2 = jnp.floor((RANK_CODE_BASE - mk2) * (1.0 / RANK_CODE_STEP) + 0.5)
                    r2_pair.append(jnp.where(mk2 < RANK_CODE_LIMIT, rank2, float(PEER_TOPK)))
                    e2_pair.append(jnp.exp(sc_ref[g2, rows, cols] - m2))
                r2_ref[h, pair_rows, cols] = jnp.concatenate(r2_pair, axis=0).astype(BF16)
                e2_ref[h, pair_rows, cols] = jnp.concatenate(e2_pair, axis=0).astype(BF16)
        return tie

    tie = lax.fori_loop(0, N_GROUPS, lambda g, t: level1(g, t, False), no_ties)
    tie = lax.fori_loop(0, PEER_HEADS, lambda h, t: level2(h, t, False), tie)

    @pl.when(jnp.max(tie) > 0.0)
    def _():
        lax.fori_loop(0, N_GROUPS, lambda g, t: level1(g, t, True), no_ties)
        lax.fori_loop(0, PEER_HEADS, lambda h, t: level2(h, t, True), no_ties)


def _peer_routing(x1b, wq, keys, tt):
    t = x1b.shape[0]
    out_blk = pl.BlockSpec((PEER_HEADS, N_KEYS, tt), lambda i: (0, 0, i))
    sds = lambda dt: jax.ShapeDtypeStruct((PEER_HEADS, N_KEYS, t), dt)
    return pl.pallas_call(
        _routing_kernel,
        grid=(t // tt,),
        in_specs=[pl.BlockSpec((tt, D_MODEL), lambda i: (i, 0)),
                  pl.BlockSpec((D_MODEL, N_GROUPS * PEER_HALF), lambda i: (0, 0)),
                  pl.BlockSpec((N_GROUPS, N_KEYS, PEER_HALF), lambda i: (0, 0, 0))],
        out_specs=[out_blk] * 4,
        out_shape=[sds(F32), sds(F32), sds(BF16), sds(BF16)],
        scratch_shapes=[pltpu.VMEM((tt, N_GROUPS * PEER_HALF), BF16),
                        pltpu.VMEM((N_GROUPS, N_KEYS, tt), F32),
                        pltpu.VMEM((N_GROUPS, N_KEYS, tt), F32),
                        pltpu.VMEM((N_GROUPS, PEER_TOPK, tt), F32)],
        compiler_params=_compiler_params(("parallel",)),
        name="peer_routing",
    )(x1b, wq, keys)


BF16_ROWS = 2 * SUBLANES


DENSE_SUB = 512


def _dense_kernel(xt_ref, u_ref, vt_ref, e1_ref, c1_ref, e2_in_ref, r2_in_ref, o_ref,
                  acc_ref, ht_ref, act_ref, e2_ref, r2_ref):
    j = pl.program_id(1)
    last = pl.num_programs(1) - 1
    _, eb, tt = ht_ref.shape
    n_sub = eb // DENSE_SUB
    zero = jnp.zeros((), BF16)
    write_slot = j % 2
    read_slot = 1 - write_slot

    def sub_rows(sb):
        return slice(sb * DENSE_SUB, (sb + 1) * DENSE_SUB)

    def stage_a(sb):
        ht_ref[write_slot, sub_rows(sb), :] = jnp.dot(u_ref[sub_rows(sb), :], xt_ref[...],
                                                      preferred_element_type=F32)

    def stage_b(sb):
        for il in range(sb * DENSE_SUB // N_KEYS, (sb + 1) * DENSE_SUB // N_KEYS):
            for tc in range(tt // LANES):
                cols = slice(tc * LANES, (tc + 1) * LANES)
                c1 = [jnp.broadcast_to(c1_ref[h, il:il + 1, cols], (BF16_ROWS, LANES)).astype(BF16)
                      for h in range(PEER_HEADS)]
                e1 = [jnp.broadcast_to(e1_ref[h, il:il + 1, cols], (BF16_ROWS, LANES)).astype(BF16)
                      for h in range(PEER_HEADS)]
                for p in range(N_KEYS // BF16_ROWS):
                    keys = slice(p * BF16_ROWS, (p + 1) * BF16_ROWS)
                    rows = slice(il * N_KEYS + p * BF16_ROWS, il * N_KEYS + (p + 1) * BF16_ROWS)
                    gate = jnp.zeros((BF16_ROWS, LANES), BF16)
                    for h in range(PEER_HEADS):
                        picked = jnp.where(r2_ref[h, keys, cols] < c1[h], e2_ref[h, keys, cols], zero)
                        gate = gate + picked * e1[h]
                    hh = ht_ref[read_slot, rows, cols]
                    gelu2 = hh * (1.0 + lax.erf(hh * (2.0 ** -0.5)))
                    act_ref[rows, cols] = gelu2.astype(BF16) * gate
        acc_ref[...] += jnp.dot(vt_ref[0, :, sub_rows(sb)], act_ref[sub_rows(sb), :],
                                preferred_element_type=F32)

    @pl.when(j == 0)
    def _():
        acc_ref[...] = jnp.zeros_like(acc_ref)
        e2_ref[...] = e2_in_ref[...]
        r2_ref[...] = r2_in_ref[...]
        for sb in range(n_sub):
            stage_a(sb)

    @pl.when(jnp.logical_and(j > 0, j < last))
    def _():
        stage_a(0)
        for sb in range(n_sub):
            if sb + 1 < n_sub:
                stage_a(sb + 1)
            stage_b(sb)

    @pl.when(j == last)
    def _():
        for sb in range(n_sub):
            stage_b(sb)
        o_ref[...] = acc_ref[...].T


def _peer_dense(x1bt, u, vt, e1, c1, e2, r2, tt, eb):
    t = x1bt.shape[1]
    n_blocks = N_EXPERTS // eb
    route = pl.BlockSpec((PEER_HEADS, N_KEYS, tt), lambda i, j: (0, 0, i),
                         pipeline_mode=pl.Buffered(1))
    route_rows = pl.BlockSpec((PEER_HEADS, eb // N_KEYS, tt),
                              lambda i, j: (0, jnp.maximum(j - 1, 0), i))
    return pl.pallas_call(
        _dense_kernel,
        grid=(t // tt, n_blocks + 1),
        in_specs=[pl.BlockSpec((D_MODEL, tt), lambda i, j: (0, i)),
                  pl.BlockSpec((eb, D_MODEL), lambda i, j: (jnp.minimum(j, n_blocks - 1), 0)),
                  pl.BlockSpec((1, D_MODEL, eb), lambda i, j: (jnp.maximum(j - 1, 0), 0, 0)),
                  route_rows, route_rows, route, route],
        out_specs=pl.BlockSpec((tt, D_MODEL), lambda i, j: (i, 0)),
        out_shape=jax.ShapeDtypeStruct((t, D_MODEL), F32),
        scratch_shapes=[pltpu.VMEM((D_MODEL, tt), F32),
                        pltpu.VMEM((2, eb, tt), F32),
                        pltpu.VMEM((eb, tt), BF16),
                        pltpu.VMEM((PEER_HEADS, N_KEYS, tt), BF16),
                        pltpu.VMEM((PEER_HEADS, N_KEYS, tt), BF16)],
        compiler_params=_compiler_params(("parallel", "arbitrary")),
        name="peer_dense",
    )(x1bt, u, vt, e1, c1, e2, r2)


def _final_kernel(x1_ref, peer_ref, p_ref, wg_ref, wp_ref, g_ref, b_ref, o_ref):
    r = DEEPNORM_ALPHA * x1_ref[...] + peer_ref[...]
    gate = jax.nn.sigmoid(jnp.dot(r.astype(BF16), wg_ref[...], preferred_element_type=F32))
    emb = jnp.dot(p_ref[...].astype(BF16), wp_ref[...], preferred_element_type=F32)
    o_ref[...] = _layer_norm(r + gate * emb, g_ref[...], b_ref[...])


def _final(x1, peer, p2, wg, wp, g, b, tm):
    t = x1.shape[0]
    row = lambda i: (i, 0)
    fixed = lambda i: (0, 0)
    return pl.pallas_call(
        _final_kernel,
        grid=(t // tm,),
        in_specs=[pl.BlockSpec((tm, D_MODEL), row), pl.BlockSpec((tm, D_MODEL), row),
                  pl.BlockSpec((tm, PLE_DIM), row),
                  pl.BlockSpec((D_MODEL, D_MODEL), fixed), pl.BlockSpec((PLE_DIM, D_MODEL), fixed),
                  pl.BlockSpec((1, D_MODEL), fixed), pl.BlockSpec((1, D_MODEL), fixed)],
        out_specs=pl.BlockSpec((tm, D_MODEL), row),
        out_shape=jax.ShapeDtypeStruct((t, D_MODEL), F32),
        compiler_params=_compiler_params(("parallel",)),
        name="final",
    )(x1, peer, p2, wg, wp, g, b)


ROW_TILE = 512
ROUTE_TILE = 512
DENSE_TOKENS = 512
DENSE_EXPERTS = 1024


def _rope_lane_table():
    inv_freq = ROPE_THETA ** (-jnp.arange(ROPE_HALF, dtype=F32) * (2.0 / ROPE_DIM))
    lane = jnp.arange(LANES) % HEAD_DIM
    return jnp.where(lane < ROPE_DIM, inv_freq[lane % ROPE_HALF], 0.0).reshape(1, LANES).astype(F32)


def _dup_kv(w):
    h0, h1 = w[:, :HEAD_DIM], w[:, HEAD_DIM:]
    return jnp.concatenate([h0, h0, h1, h1], axis=1)


def kernel(x, p, positions, w_in, sinks, conv_w, conv_b, conv_ln_g, conv_ln_b, w_out, ln1_g, ln1_b,
           peer_query, peer_keys, peer_u, peer_v, ple_proj, ple_gate, ln2_g, ln2_b):
    b, s, d = x.shape
    t = b * s
    pos = positions.reshape(t, 1).astype(F32)
    invf = _rope_lane_table()
    xc = x.reshape(t, d)
    vec = lambda a: a.reshape(1, -1)
    for i in range(DEPTH):
        w = w_in[i]
        o = ATTN_WIDTH
        w_cat = jnp.concatenate(
            [w[:, :o], _dup_kv(w[:, o:o + KV_WIDTH]), _dup_kv(w[:, o + KV_WIDTH:o + 2 * KV_WIDTH]),
             w[:, o + 2 * KV_WIDTH:]], axis=1).astype(BF16)
        q, k, v, glu = _in_proj(xc, w_cat, pos, invf, ROW_TILE)
        att = _attention(sinks[i], q.reshape(b, s, -1), k.reshape(b, s, -1), v.reshape(b, s, -1))
        cnv = _conv_group(glu.reshape(b, s, -1), conv_w[i], vec(conv_b[i]),
                          vec(conv_ln_g[i]), vec(conv_ln_b[i]))
        wo = w_out[i].astype(BF16)
        x1, x1b, x1bt = _out_proj(att.reshape(t, -1), cnv.reshape(t, -1), xc, wo[:ATTN_WIDTH],
                            wo[ATTN_WIDTH:], vec(ln1_g[i]), vec(ln1_b[i]), ROW_TILE)
        keys = peer_keys[i].reshape(N_GROUPS, N_KEYS, PEER_HALF).astype(BF16)
        e1, c1, e2, r2 = _peer_routing(x1b, peer_query[i].astype(BF16), keys, ROUTE_TILE)
        vt = peer_v[i].astype(BF16).reshape(N_EXPERTS // DENSE_EXPERTS, DENSE_EXPERTS, d).transpose(0, 2, 1)
        peer = _peer_dense(x1bt, peer_u[i].astype(BF16), vt, e1, c1, e2, r2,
                           DENSE_TOKENS, DENSE_EXPERTS)
        xc = _final(x1, peer, p[i].reshape(t, -1), ple_gate[i].astype(BF16),
                    ple_proj[i].astype(BF16), vec(ln2_g[i]), vec(ln2_b[i]), ROW_TILE)
    return xc.reshape(b, s, d)
```

```python
import jax
import jax.numpy as jnp
from jax import lax
from jax.experimental import pallas as pl
from jax.experimental.pallas import tpu as pltpu

F32 = jnp.float32
BF16 = jnp.bfloat16

D_MODEL = 1024
N_Q_HEADS = 8
N_KV_HEADS = 2
HEAD_DIM = 64
ATTN_WIDTH = N_Q_HEADS * HEAD_DIM
KV_WIDTH = N_KV_HEADS * HEAD_DIM
WINDOW = 128
BLOCK = 128
ROPE_THETA = 500000.0
ROPE_DIM = HEAD_DIM // 4
ROPE_HALF = ROPE_DIM // 2
CONV_WIDTH = D_MODEL - ATTN_WIDTH
CONV_TAPS = 31
PEER_HEADS = 8
N_KEYS = 128
N_EXPERTS = N_KEYS * N_KEYS
PEER_HALF = 128
PEER_TOPK = 16
PLE_DIM = 256
DEPTH = 1
DEEPNORM_ALPHA = (2 * DEPTH) ** 0.25
LN_EPS = 1e-5
NEG_INF = -1e30

LANES = 128
SUBLANES = 8
VMEM_LIMIT_BYTES = 56 * 1024 * 1024

RANK_CODE_BASE = -(2.0 ** 127)
RANK_CODE_STEP = 2.0 ** 120
RANK_CODE_LIMIT = -1.5 * 2.0 ** 126
PAD_SCORE = -(2.0 ** 126)


def _compiler_params(semantics, flags=None):
    return pltpu.CompilerParams(dimension_semantics=semantics,
                                vmem_limit_bytes=VMEM_LIMIT_BYTES, flags=flags)


def _layer_norm(v, g, b):
    mu = jnp.mean(v, axis=-1, keepdims=True)
    d = v - mu
    var = jnp.mean(d * d, axis=-1, keepdims=True)
    return d * lax.rsqrt(var + LN_EPS) * g + b


QK_GROUPS = (ATTN_WIDTH + 2 * KV_WIDTH) // LANES


def _inproj_kernel(x_ref, w_ref, pos_ref, invf_ref, q_ref, k_ref, v_ref, glu_ref):
    h = jnp.dot(x_ref[...].astype(BF16), w_ref[...], preferred_element_type=F32)
    tm = h.shape[0]
    ang = pos_ref[...] * invf_ref[...]
    lane = lax.broadcasted_iota(jnp.int32, (tm, LANES), 1) % HEAD_DIM
    first = lane < ROPE_HALF
    second = jnp.logical_and(lane >= ROPE_HALF, lane < ROPE_DIM)
    cos = jnp.cos(ang)
    sin = jnp.sin(ang)
    sin_signed = jnp.where(first, -sin, jnp.where(second, sin, 0.0))

    def rotate(t):
        up = pltpu.roll(t, LANES - ROPE_HALF, 1)
        dn = pltpu.roll(t, ROPE_HALF, 1)
        return t * cos + jnp.where(first, up, dn) * sin_signed

    scale = HEAD_DIM ** -0.5
    for g in range(ATTN_WIDTH // LANES):
        q_ref[:, g * LANES:(g + 1) * LANES] = (
            rotate(h[:, g * LANES:(g + 1) * LANES]) * scale).astype(BF16)
    o = ATTN_WIDTH
    for g in range(2 * KV_WIDTH // LANES):
        k_ref[:, g * LANES:(g + 1) * LANES] = rotate(
            h[:, o + g * LANES:o + (g + 1) * LANES]).astype(BF16)
    o += 2 * KV_WIDTH
    v_ref[...] = h[:, o:o + 2 * KV_WIDTH].astype(BF16)
    o += 2 * KV_WIDTH
    glu_ref[...] = h[:, o:o + CONV_WIDTH] * jax.nn.sigmoid(
        h[:, o + CONV_WIDTH:o + 2 * CONV_WIDTH])


def _in_proj(x2, w_cat, pos, invf, tm):
    t = x2.shape[0]
    n_in = w_cat.shape[1]
    row = lambda i: (i, 0)
    fixed = lambda i: (0, 0)
    return pl.pallas_call(
        _inproj_kernel,
        grid=(t // tm,),
        in_specs=[pl.BlockSpec((tm, D_MODEL), row),
                  pl.BlockSpec((D_MODEL, n_in), fixed),
                  pl.BlockSpec((tm, 1), row),
                  pl.BlockSpec((1, LANES), fixed)],
        out_specs=[pl.BlockSpec((tm, ATTN_WIDTH), row),
                   pl.BlockSpec((tm, 2 * KV_WIDTH), row),
                   pl.BlockSpec((tm, 2 * KV_WIDTH), row),
                   pl.BlockSpec((tm, CONV_WIDTH), row)],
        out_shape=[jax.ShapeDtypeStruct((t, ATTN_WIDTH), BF16),
                   jax.ShapeDtypeStruct((t, 2 * KV_WIDTH), BF16),
                   jax.ShapeDtypeStruct((t, 2 * KV_WIDTH), BF16),
                   jax.ShapeDtypeStruct((t, CONV_WIDTH), F32)],
        compiler_params=_compiler_params(("parallel",)),
        name="in_proj",
    )(x2, w_cat, pos, invf)


def _attn_kernel(sink_ref, q_ref, k_ref, v_ref, o_ref):
    s_len = q_ref.shape[1]
    nb = s_len // BLOCK
    qi = lax.broadcasted_iota(jnp.int32, (BLOCK, 2 * BLOCK), 0)
    si = lax.broadcasted_iota(jnp.int32, (BLOCK, 2 * BLOCK), 1)
    band = jnp.logical_and(si > qi, si <= qi + WINDOW)
    lane = lax.broadcasted_iota(jnp.int32, (2 * BLOCK, LANES), 1)
    low = lane < HEAD_DIM
    heads_per_group = N_Q_HEADS // N_KV_HEADS
    zero = jnp.zeros((), BF16)

    def body(n, carry):
        start = pl.multiple_of(n * BLOCK, BLOCK)
        prev = pl.multiple_of(jnp.maximum(n - 1, 0) * BLOCK, BLOCK)
        valid = jnp.logical_and(band, si >= jnp.where(n == 0, BLOCK, 0))
        q = q_ref[0, pl.ds(start, BLOCK), :]
        k2 = jnp.concatenate([k_ref[0, pl.ds(prev, BLOCK), :],
                              k_ref[0, pl.ds(start, BLOCK), :]], axis=0)
        v2 = jnp.concatenate([v_ref[0, pl.ds(prev, BLOCK), :],
                              v_ref[0, pl.ds(start, BLOCK), :]], axis=0)
        for pair in range(N_Q_HEADS // 2):
            g = (2 * pair) // heads_per_group
            qp = q[:, pair * LANES:(pair + 1) * LANES]
            kg = k2[:, g * LANES:(g + 1) * LANES]
            vg = v2[:, g * LANES:(g + 1) * LANES]
            out = jnp.zeros((BLOCK, LANES), F32)
            for half in range(2):
                keep = low if half == 0 else jnp.logical_not(low)
                kx = jnp.where(keep, kg, zero)
                vx = jnp.where(keep, vg, zero)
                sink = sink_ref[2 * pair + half]
                s = lax.dot_general(qp, kx, (((1,), (1,)), ((), ())),
                                    preferred_element_type=F32)
                s = jnp.where(valid, s, NEG_INF)
                m = jnp.maximum(jnp.max(s, axis=-1, keepdims=True), sink)
                e = jnp.exp(s - m)
                denom = jnp.sum(e, axis=-1, keepdims=True) + jnp.exp(sink - m)
                pv = jnp.dot(e.astype(BF16), vx, preferred_element_type=F32)
                out = out + pv * (1.0 / denom)
            o_ref[0, pl.ds(start, BLOCK), pair * LANES:(pair + 1) * LANES] = out.astype(BF16)
        return carry

    lax.fori_loop(0, nb, body, 0)


def _attention(sinks, q3, k3, v3):
    b, s, _ = q3.shape
    blk = lambda w: pl.BlockSpec((1, s, w), lambda i: (i, 0, 0))
    return pl.pallas_call(
        _attn_kernel,
        grid=(b,),
        in_specs=[pl.BlockSpec(memory_space=pltpu.SMEM),
                  blk(ATTN_WIDTH), blk(2 * KV_WIDTH), blk(2 * KV_WIDTH)],
        out_specs=blk(ATTN_WIDTH),
        out_shape=jax.ShapeDtypeStruct((b, s, ATTN_WIDTH), BF16),
        compiler_params=_compiler_params(("parallel",)),
        name="attention",
    )(sinks, q3, k3, v3)


CONV_PAD = 32
CONV_ROWS = 128
CONV_COPY_ROWS = 256


def _conv_kernel(h_ref, w_ref, b_ref, g_ref, beta_ref, o_ref, sh_ref, y_ref):
    s_len = h_ref.shape[1]
    padded = CONV_PAD + s_len
    for l in range(CONV_WIDTH // LANES):
        cols = slice(l * LANES, (l + 1) * LANES)
        sh_ref[0, 0:CONV_PAD, :] = jnp.zeros((CONV_PAD, LANES), F32)
        sh_ref[0, CONV_PAD:, :] = h_ref[0, :, cols]
        for r in range(1, SUBLANES):
            for lo in range(SUBLANES, padded, CONV_COPY_ROWS):
                hi = min(lo + CONV_COPY_ROWS, padded)
                sh_ref[r, lo:hi, :] = sh_ref[0, lo - r:hi - r, :]

        def body(c, carry):
            base = pl.multiple_of(CONV_PAD + c * CONV_ROWS, SUBLANES)
            acc = jnp.zeros((CONV_ROWS, LANES), F32) + b_ref[:, cols]
            for j in range(CONV_TAPS):
                a, r = divmod(CONV_TAPS - 1 - j, SUBLANES)
                acc = acc + sh_ref[r, pl.ds(base - SUBLANES * a, CONV_ROWS), :] * w_ref[j:j + 1, cols]
            y_ref[pl.ds(pl.multiple_of(c * CONV_ROWS, SUBLANES), CONV_ROWS), cols] = acc
            return carry

        lax.fori_loop(0, s_len // CONV_ROWS, body, 0)

    def norm(c, carry):
        rows = pl.ds(pl.multiple_of(c * CONV_ROWS, SUBLANES), CONV_ROWS)
        y = _layer_norm(y_ref[rows, :], g_ref[...], beta_ref[...])
        o_ref[0, rows, :] = (y * jax.nn.sigmoid(y)).astype(BF16)
        return carry

    lax.fori_loop(0, s_len // CONV_ROWS, norm, 0)


def _conv_group(h3, conv_w, conv_b, ln_g, ln_b):
    b, s, _ = h3.shape
    blk = pl.BlockSpec((1, s, CONV_WIDTH), lambda i: (i, 0, 0))
    vec = pl.BlockSpec((1, CONV_WIDTH), lambda i: (0, 0))
    return pl.pallas_call(
        _conv_kernel,
        grid=(b,),
        in_specs=[blk, pl.BlockSpec((CONV_TAPS, CONV_WIDTH), lambda i: (0, 0)), vec, vec, vec],
        out_specs=blk,
        out_shape=jax.ShapeDtypeStruct((b, s, CONV_WIDTH), BF16),
        scratch_shapes=[pltpu.VMEM((SUBLANES, CONV_PAD + s, LANES), F32),
                        pltpu.VMEM((s, CONV_WIDTH), F32)],
        compiler_params=_compiler_params(("parallel",)),
        name="conv_group",
    )(h3, conv_w, conv_b, ln_g, ln_b)


def _outproj_kernel(att_ref, cnv_ref, x_ref, wa_ref, wc_ref, g_ref, b_ref, y_ref, yb_ref, ybt_ref):
    mixed = jnp.dot(att_ref[...], wa_ref[...], preferred_element_type=F32)
    mixed = mixed + jnp.dot(cnv_ref[...], wc_ref[...], preferred_element_type=F32)
    y = _layer_norm(DEEPNORM_ALPHA * x_ref[...] + mixed, g_ref[...], b_ref[...])
    y_ref[...] = y
    yb_ref[...] = y.astype(BF16)
    ybt_ref[...] = y.T.astype(BF16)


def _out_proj(att, cnv, x2, wa, wc, g, b, tm):
    t = x2.shape[0]
    row = lambda i: (i, 0)
    fixed = lambda i: (0, 0)
    return pl.pallas_call(
        _outproj_kernel,
        grid=(t // tm,),
        in_specs=[pl.BlockSpec((tm, ATTN_WIDTH), row), pl.BlockSpec((tm, CONV_WIDTH), row),
                  pl.BlockSpec((tm, D_MODEL), row),
                  pl.BlockSpec((ATTN_WIDTH, D_MODEL), fixed),
                  pl.BlockSpec((CONV_WIDTH, D_MODEL), fixed),
                  pl.BlockSpec((1, D_MODEL), fixed), pl.BlockSpec((1, D_MODEL), fixed)],
        out_specs=[pl.BlockSpec((tm, D_MODEL), row), pl.BlockSpec((tm, D_MODEL), row),
                   pl.BlockSpec((D_MODEL, tm), lambda i: (0, i))],
        out_shape=[jax.ShapeDtypeStruct((t, D_MODEL), F32),
                   jax.ShapeDtypeStruct((t, D_MODEL), BF16),
                   jax.ShapeDtypeStruct((D_MODEL, t), BF16)],
        compiler_params=_compiler_params(("parallel",)),
        name="out_proj",
    )(att, cnv, x2, wa, wc, g, b)


N_GROUPS = 2 * PEER_HEADS
KEY_VREGS = N_KEYS // SUBLANES


def _tree_argmax(vals, bases):
    items = [(v, float(b)) for v, b in zip(vals, bases)]
    while len(items) > 1:
        nxt = []
        for i in range(0, len(items) - 1, 2):
            (va, ba), (vb, bb) = items[i], items[i + 1]
            better = vb > va
            nxt.append((jnp.where(better, vb, va), jnp.where(better, bb, ba)))
        if len(items) % 2:
            nxt.append(items[-1])
        items = nxt
    return items[0]


def _tree_max(vals):
    items = list(vals)
    while len(items) > 1:
        nxt = [jnp.maximum(items[i], items[i + 1]) for i in range(0, len(items) - 1, 2)]
        if len(items) % 2:
            nxt.append(items[-1])
        items = nxt
    return items[0]


def _extract_topk(vals, bases, subf, rounds, break_ties):
    vals = list(vals)
    tops = []
    for a in range(rounds):
        code = RANK_CODE_BASE - a * RANK_CODE_STEP
        if break_ties:
            best, best_base = _tree_argmax(vals, bases)
            top = jnp.max(best, axis=0, keepdims=True)
            order = jnp.where(best == top, best_base + subf, 1.0e9)
            winner = jnp.min(order, axis=0, keepdims=True)
            winner_base = winner - subf
            vals = [jnp.where(winner_base == float(bases[k]), code, vals[k])
                    for k in range(len(vals))]
        else:
            top = jnp.max(_tree_max(vals), axis=0, keepdims=True)
            vals = [jnp.where(v == top, code, v) for v in vals]
        tops.append(top)
    return vals, tops


def _tie_flag(marked, rounds):
    count = jnp.zeros(marked[0].shape, F32)
    for m in marked:
        count = count + jnp.where(m < RANK_CODE_LIMIT, 1.0, 0.0)
    return jnp.where(jnp.sum(count, axis=0, keepdims=True) == float(rounds), 0.0, 1.0)


PAIR_VREGS = ((0, 0, 8, 0), (0, 8, 8, 8), (1, 0, 8, 16), (2, 0, 5, 32), (3, 0, 4, 48),
              (4, 0, 3, 64), (5, 0, 2, 80), (6, 0, 2, 96), (7, 0, 2, 112))
PAIR_TAIL_BASE = 128


def _routing_kernel(x_ref, wq_ref, keys_ref, e1_ref, c1_ref, e2_ref, r2_ref,
                    qb_ref, sc_ref, mk_ref, top_ref):
    tt = x_ref.shape[0]
    n_chunks = tt // LANES
    qb_ref[...] = jnp.dot(x_ref[...], wq_ref[...], preferred_element_type=F32).astype(BF16)
    for g in range(N_GROUPS):
        sc_ref[g] = lax.dot_general(keys_ref[g], qb_ref[:, g * PEER_HALF:(g + 1) * PEER_HALF],
                                    (((1,), (1,)), ((), ())), preferred_element_type=F32)

    sub = lax.broadcasted_iota(jnp.int32, (SUBLANES, LANES), 0)
    subf = sub.astype(F32)
    key_bases = tuple(SUBLANES * k for k in range(KEY_VREGS))

    pair_bases = tuple(v[3] for v in PAIR_VREGS) + (PAIR_TAIL_BASE,)
    no_ties = jnp.zeros((1, LANES), F32)

    def level1(g, tie, break_ties):
        for c in range(n_chunks):
            cols = slice(c * LANES, (c + 1) * LANES)
            vals = [sc_ref[g, SUBLANES * k:SUBLANES * (k + 1), cols] for k in range(KEY_VREGS)]
            marked, tops = _extract_topk(vals, key_bases, subf, PEER_TOPK, break_ties)
            if not break_ties:
                tie = jnp.maximum(tie, _tie_flag(marked, PEER_TOPK))
            for k in range(KEY_VREGS):
                mk_ref[g, SUBLANES * k:SUBLANES * (k + 1), cols] = marked[k]
            for a in range(PEER_TOPK):
                top_ref[g, a:a + 1, cols] = tops[a]
        return tie

    def level2(h, tie, break_ties):
        g1 = 2 * h
        g2 = 2 * h + 1
        for c in range(n_chunks):
            cols = slice(c * LANES, (c + 1) * LANES)
            v2 = (top_ref[g2, 0:SUBLANES, cols], top_ref[g2, SUBLANES:2 * SUBLANES, cols])
            m1 = top_ref[g1, 0:1, cols]
            m2 = top_ref[g2, 0:1, cols]
            cands = []
            for a, b0, used, _ in PAIR_VREGS:
                cand = top_ref[g1, a:a + 1, cols] + v2[b0 // SUBLANES]
                cands.append(cand if used == SUBLANES else jnp.where(sub < used, cand, PAD_SCORE))
            cands.append(top_ref[g1, SUBLANES:2 * SUBLANES, cols] + m2)
            marked, _ = _extract_topk(cands, pair_bases, subf, PEER_TOPK, break_ties)
            if not break_ties:
                tie = jnp.maximum(tie, _tie_flag(marked, PEER_TOPK))
            best = m1 + m2
            picked = [m < RANK_CODE_LIMIT for m in marked]
            ones = [jnp.where(p, 1.0, 0.0) for p in picked]
            zsum = jnp.zeros((SUBLANES, LANES), F32)
            for p, cand in zip(picked, cands):
                zsum = zsum + jnp.where(p, jnp.exp(cand - best), 0.0)
            inv_z = 0.5 / jnp.sum(zsum, axis=0, keepdims=True)
            widths = [jnp.sum(ones[0] + ones[1], axis=0, keepdims=True)]
            widths += [jnp.sum(ones[j], axis=0, keepdims=True) for j in range(2, len(PAIR_VREGS))]
            widths += [ones[-1][s:s + 1, :] for s in range(SUBLANES)]
            widths = [jnp.broadcast_to(w, (SUBLANES, LANES)) for w in widths]
            for kk in range(KEY_VREGS // 2):
                pair_rows = slice(2 * SUBLANES * kk, 2 * SUBLANES * (kk + 1))
                e2_pair, r2_pair = [], []
                for k in (2 * kk, 2 * kk + 1):
                    rows = slice(SUBLANES * k, SUBLANES * (k + 1))
                    mk1 = mk_ref[g1, rows, cols]
                    cnt1 = jnp.zeros((SUBLANES, LANES), F32)
                    for a in range(PEER_TOPK):
                        cnt1 = jnp.where(mk1 == RANK_CODE_BASE - a * RANK_CODE_STEP, widths[a], cnt1)
                    c1_ref[h, rows, cols] = cnt1
                    e1_ref[h, rows, cols] = jnp.exp(sc_ref[g1, rows, cols] - m1) * inv_z
                    mk2 = mk_ref[g2, rows, cols]
                    rank2 = jnp.floor((RANK_CODE_BASE - mk2) * (1.0 / RANK_CODE_STEP) + 0.5)
                    r2_pair.append(jnp.where(mk2 < RANK_CODE_LIMIT, rank2, float(PEER_TOPK)))
                    e2_pair.append(jnp.exp(sc_ref[g2, rows, cols] - m2))
                r2_ref[h, pair_rows, cols] = jnp.concatenate(r2_pair, axis=0).astype(BF16)
                e2_ref[h, pair_rows, cols] = jnp.concatenate(e2_pair, axis=0).astype(BF16)
        return tie

    tie = lax.fori_loop(0, N_GROUPS, lambda g, t: level1(g, t, False), no_ties)
    tie = lax.fori_loop(0, PEER_HEADS, lambda h, t: level2(h, t, False), tie)

    @pl.when(jnp.max(tie) > 0.0)
    def _():
        lax.fori_loop(0, N_GROUPS, lambda g, t: level1(g, t, True), no_ties)
        lax.fori_loop(0, PEER_HEADS, lambda h, t: level2(h, t, True), no_ties)


def _peer_routing(x1b, wq, keys, tt):
    t = x1b.shape[0]
    out_blk = pl.BlockSpec((PEER_HEADS, N_KEYS, tt), lambda i: (0, 0, i))
    sds = lambda dt: jax.ShapeDtypeStruct((PEER_HEADS, N_KEYS, t), dt)
    return pl.pallas_call(
        _routing_kernel,
        grid=(t // tt,),
        in_specs=[pl.BlockSpec((tt, D_MODEL), lambda i: (i, 0)),
                  pl.BlockSpec((D_MODEL, N_GROUPS * PEER_HALF), lambda i: (0, 0)),
                  pl.BlockSpec((N_GROUPS, N_KEYS, PEER_HALF), lambda i: (0, 0, 0))],
        out_specs=[out_blk] * 4,
        out_shape=[sds(F32), sds(F32), sds(BF16), sds(BF16)],
        scratch_shapes=[pltpu.VMEM((tt, N_GROUPS * PEER_HALF), BF16),
                        pltpu.VMEM((N_GROUPS, N_KEYS, tt), F32),
                        pltpu.VMEM((N_GROUPS, N_KEYS, tt), F32),
                        pltpu.VMEM((N_GROUPS, PEER_TOPK, tt), F32)],
        compiler_params=_compiler_params(("parallel",)),
        name="peer_routing",
    )(x1b, wq, keys)


BF16_ROWS = 2 * SUBLANES


DENSE_SUB = 512


def _dense_kernel(xt_ref, u_ref, vt_ref, e1_ref, c1_ref, e2_in_ref, r2_in_ref, o_ref,
                  acc_ref, ht_ref, act_ref, e2_ref, r2_ref):
    j = pl.program_id(1)
    last = pl.num_programs(1) - 1
    _, eb, tt = ht_ref.shape
    n_sub = eb // DENSE_SUB
    zero = jnp.zeros((), BF16)
    write_slot = j % 2
    read_slot = 1 - write_slot

    def sub_rows(sb):
        return slice(sb * DENSE_SUB, (sb + 1) * DENSE_SUB)

    def stage_a(sb):
        ht_ref[write_slot, sub_rows(sb), :] = jnp.dot(u_ref[sub_rows(sb), :], xt_ref[...],
                                                      preferred_element_type=F32)

    def stage_b(sb):
        for il in range(sb * DENSE_SUB // N_KEYS, (sb + 1) * DENSE_SUB // N_KEYS):
            for tc in range(tt // LANES):
                cols = slice(tc * LANES, (tc + 1) * LANES)
                c1 = [jnp.broadcast_to(c1_ref[h, il:il + 1, cols], (BF16_ROWS, LANES)).astype(BF16)
                      for h in range(PEER_HEADS)]
                e1 = [jnp.broadcast_to(e1_ref[h, il:il + 1, cols], (BF16_ROWS, LANES)).astype(BF16)
                      for h in range(PEER_HEADS)]
                for p in range(N_KEYS // BF16_ROWS):
                    keys = slice(p * BF16_ROWS, (p + 1) * BF16_ROWS)
                    rows = slice(il * N_KEYS + p * BF16_ROWS, il * N_KEYS + (p + 1) * BF16_ROWS)
                    gate = jnp.zeros((BF16_ROWS, LANES), BF16)
                    for h in range(PEER_HEADS):
                        picked = jnp.where(r2_ref[h, keys, cols] < c1[h], e2_ref[h, keys, cols], zero)
                        gate = gate + picked * e1[h]
                    hh = ht_ref[read_slot, rows, cols]
                    gelu2 = hh * (1.0 + lax.erf(hh * (2.0 ** -0.5)))
                    act_ref[rows, cols] = gelu2.astype(BF16) * gate
        acc_ref[...] += jnp.dot(vt_ref[:, sub_rows(sb)], act_ref[sub_rows(sb), :],
                                preferred_element_type=F32)

    @pl.when(j == 0)
    def _():
        acc_ref[...] = jnp.zeros_like(acc_ref)
        e2_ref[...] = e2_in_ref[...]
        r2_ref[...] = r2_in_ref[...]
        for sb in range(n_sub):
            stage_a(sb)

    @pl.when(jnp.logical_and(j > 0, j < last))
    def _():
        stage_a(0)
        for sb in range(n_sub):
            if sb + 1 < n_sub:
                stage_a(sb + 1)
            stage_b(sb)

    @pl.when(j == last)
    def _():
        for sb in range(n_sub):
            stage_b(sb)
        o_ref[...] = acc_ref[...].T


def _peer_dense(x1bt, u, vt, e1, c1, e2, r2, tt, eb):
    t = x1bt.shape[1]
    n_blocks = N_EXPERTS // eb
    route = pl.BlockSpec((PEER_HEADS, N_KEYS, tt), lambda i, j: (0, 0, i))
    route_rows = pl.BlockSpec((PEER_HEADS, eb // N_KEYS, tt),
                              lambda i, j: (0, jnp.maximum(j - 1, 0), i))
    return pl.pallas_call(
        _dense_kernel,
        grid=(t // tt, n_blocks + 1),
        in_specs=[pl.BlockSpec((D_MODEL, tt), lambda i, j: (0, i)),
                  pl.BlockSpec((eb, D_MODEL), lambda i, j: (jnp.minimum(j, n_blocks - 1), 0)),
                  pl.BlockSpec((D_MODEL, eb), lambda i, j: (0, jnp.maximum(j - 1, 0))),
                  route_rows, route_rows, route, route],
        out_specs=pl.BlockSpec((tt, D_MODEL), lambda i, j: (i, 0)),
        out_shape=jax.ShapeDtypeStruct((t, D_MODEL), F32),
        scratch_shapes=[pltpu.VMEM((D_MODEL, tt), F32),
                        pltpu.VMEM((2, eb, tt), F32),
                        pltpu.VMEM((eb, tt), BF16),
                        pltpu.VMEM((PEER_HEADS, N_KEYS, tt), BF16),
                        pltpu.VMEM((PEER_HEADS, N_KEYS, tt), BF16)],
        compiler_params=_compiler_params(("parallel", "arbitrary")),
        name="peer_dense",
    )(x1bt, u, vt, e1, c1, e2, r2)


def _final_kernel(x1_ref, peer_ref, p_ref, wg_ref, wp_ref, g_ref, b_ref, o_ref):
    r = DEEPNORM_ALPHA * x1_ref[...] + peer_ref[...]
    gate = jax.nn.sigmoid(jnp.dot(r.astype(BF16), wg_ref[...], preferred_element_type=F32))
    emb = jnp.dot(p_ref[...].astype(BF16), wp_ref[...], preferred_element_type=F32)
    o_ref[...] = _layer_norm(r + gate * emb, g_ref[...], b_ref[...])


def _final(x1, peer, p2, wg, wp, g, b, tm):
    t = x1.shape[0]
    row = lambda i: (i, 0)
    fixed = lambda i: (0, 0)
    return pl.pallas_call(
        _final_kernel,
        grid=(t // tm,),
        in_specs=[pl.BlockSpec((tm, D_MODEL), row), pl.BlockSpec((tm, D_MODEL), row),
                  pl.BlockSpec((tm, PLE_DIM), row),
                  pl.BlockSpec((D_MODEL, D_MODEL), fixed), pl.BlockSpec((PLE_DIM, D_MODEL), fixed),
                  pl.BlockSpec((1, D_MODEL), fixed), pl.BlockSpec((1, D_MODEL), fixed)],
        out_specs=pl.BlockSpec((tm, D_MODEL), row),
        out_shape=jax.ShapeDtypeStruct((t, D_MODEL), F32),
        compiler_params=_compiler_params(("parallel",)),
        name="final",
    )(x1, peer, p2, wg, wp, g, b)


ROW_TILE = 512
ROUTE_TILE = 256
DENSE_TOKENS = 512
DENSE_EXPERTS = 1024


def _rope_lane_table():
    inv_freq = ROPE_THETA ** (-jnp.arange(ROPE_HALF, dtype=F32) * (2.0 / ROPE_DIM))
    lane = jnp.arange(LANES) % HEAD_DIM
    return jnp.where(lane < ROPE_DIM, inv_freq[lane % ROPE_HALF], 0.0).reshape(1, LANES).astype(F32)


def _dup_kv(w):
    h0, h1 = w[:, :HEAD_DIM], w[:, HEAD_DIM:]
    return jnp.concatenate([h0, h0, h1, h1], axis=1)


def kernel(x, p, positions, w_in, sinks, conv_w, conv_b, conv_ln_g, conv_ln_b, w_out, ln1_g, ln1_b,
           peer_query, peer_keys, peer_u, peer_v, ple_proj, ple_gate, ln2_g, ln2_b):
    b, s, d = x.shape
    t = b * s
    pos = positions.reshape(t, 1).astype(F32)
    invf = _rope_lane_table()
    xc = x.reshape(t, d)
    vec = lambda a: a.reshape(1, -1)
    for i in range(DEPTH):
        w = w_in[i]
        o = ATTN_WIDTH
        w_cat = jnp.concatenate(
            [w[:, :o], _dup_kv(w[:, o:o + KV_WIDTH]), _dup_kv(w[:, o + KV_WIDTH:o + 2 * KV_WIDTH]),
             w[:, o + 2 * KV_WIDTH:]], axis=1).astype(BF16)
        q, k, v, glu = _in_proj(xc, w_cat, pos, invf, ROW_TILE)
        att = _attention(sinks[i], q.reshape(b, s, -1), k.reshape(b, s, -1), v.reshape(b, s, -1))
        cnv = _conv_group(glu.reshape(b, s, -1), conv_w[i], vec(conv_b[i]),
                          vec(conv_ln_g[i]), vec(conv_ln_b[i]))
        wo = w_out[i].astype(BF16)
        x1, x1b, x1bt = _out_proj(att.reshape(t, -1), cnv.reshape(t, -1), xc, wo[:ATTN_WIDTH],
                            wo[ATTN_WIDTH:], vec(ln1_g[i]), vec(ln1_b[i]), ROW_TILE)
        keys = peer_keys[i].reshape(N_GROUPS, N_KEYS, PEER_HALF).astype(BF16)
        e1, c1, e2, r2 = _peer_routing(x1b, peer_query[i].astype(BF16), keys, ROUTE_TILE)
        peer = _peer_dense(x1bt, peer_u[i].astype(BF16), peer_v[i].astype(BF16).T, e1, c1, e2, r2,
                           DENSE_TOKENS, DENSE_EXPERTS)
        xc = _final(x1, peer, p[i].reshape(t, -1), ple_gate[i].astype(BF16),
                    ple_proj[i].astype(BF16), vec(ln2_g[i]), vec(ln2_b[i]), ROW_TILE)
    return xc.reshape(b, s, d)
```

```python
import jax
import jax.numpy as jnp
from jax import lax
from jax.experimental import pallas as pl
from jax.experimental.pallas import tpu as pltpu

F32 = jnp.float32
BF16 = jnp.bfloat16

D_MODEL = 1024
N_Q_HEADS = 8
N_KV_HEADS = 2
HEAD_DIM = 64
ATTN_WIDTH = N_Q_HEADS * HEAD_DIM
KV_WIDTH = N_KV_HEADS * HEAD_DIM
WINDOW = 128
BLOCK = 128
ROPE_THETA = 500000.0
ROPE_DIM = HEAD_DIM // 4
ROPE_HALF = ROPE_DIM // 2
CONV_WIDTH = D_MODEL - ATTN_WIDTH
CONV_TAPS = 31
PEER_HEADS = 8
N_KEYS = 128
N_EXPERTS = N_KEYS * N_KEYS
PEER_HALF = 128
PEER_TOPK = 16
PLE_DIM = 256
DEPTH = 1
DEEPNORM_ALPHA = (2 * DEPTH) ** 0.25
LN_EPS = 1e-5
NEG_INF = -1e30

LANES = 128
SUBLANES = 8
VMEM_LIMIT_BYTES = 56 * 1024 * 1024

RANK_CODE_BASE = -(2.0 ** 127)
RANK_CODE_STEP = 2.0 ** 120
RANK_CODE_LIMIT = -1.5 * 2.0 ** 126
PAD_SCORE = -(2.0 ** 126)


def _compiler_params(semantics, flags=None):
    return pltpu.CompilerParams(dimension_semantics=semantics,
                                vmem_limit_bytes=VMEM_LIMIT_BYTES, flags=flags)


def _layer_norm(v, g, b):
    mu = jnp.mean(v, axis=-1, keepdims=True)
    d = v - mu
    var = jnp.mean(d * d, axis=-1, keepdims=True)
    return d * lax.rsqrt(var + LN_EPS) * g + b


QK_GROUPS = (ATTN_WIDTH + 2 * KV_WIDTH) // LANES


def _inproj_kernel(x_ref, w_ref, pos_ref, invf_ref, q_ref, k_ref, v_ref, glu_ref):
    h = jnp.dot(x_ref[...].astype(BF16), w_ref[...], preferred_element_type=F32)
    tm = h.shape[0]
    ang = pos_ref[...] * invf_ref[...]
    lane = lax.broadcasted_iota(jnp.int32, (tm, LANES), 1) % HEAD_DIM
    first = lane < ROPE_HALF
    second = jnp.logical_and(lane >= ROPE_HALF, lane < ROPE_DIM)
    cos = jnp.cos(ang)
    sin = jnp.sin(ang)
    sin_signed = jnp.where(first, -sin, jnp.where(second, sin, 0.0))

    def rotate(t):
        up = pltpu.roll(t, LANES - ROPE_HALF, 1)
        dn = pltpu.roll(t, ROPE_HALF, 1)
        return t * cos + jnp.where(first, up, dn) * sin_signed

    scale = HEAD_DIM ** -0.5
    for g in range(ATTN_WIDTH // LANES):
        q_ref[:, g * LANES:(g + 1) * LANES] = (
            rotate(h[:, g * LANES:(g + 1) * LANES]) * scale).astype(BF16)
    o = ATTN_WIDTH
    for g in range(2 * KV_WIDTH // LANES):
        k_ref[:, g * LANES:(g + 1) * LANES] = rotate(
            h[:, o + g * LANES:o + (g + 1) * LANES]).astype(BF16)
    o += 2 * KV_WIDTH
    v_ref[...] = h[:, o:o + 2 * KV_WIDTH].astype(BF16)
    o += 2 * KV_WIDTH
    glu_ref[...] = h[:, o:o + CONV_WIDTH] * jax.nn.sigmoid(
        h[:, o + CONV_WIDTH:o + 2 * CONV_WIDTH])


def _in_proj(x2, w_cat, pos, invf, tm):
    t = x2.shape[0]
    n_in = w_cat.shape[1]
    row = lambda i: (i, 0)
    fixed = lambda i: (0, 0)
    return pl.pallas_call(
        _inproj_kernel,
        grid=(t // tm,),
        in_specs=[pl.BlockSpec((tm, D_MODEL), row),
                  pl.BlockSpec((D_MODEL, n_in), fixed),
                  pl.BlockSpec((tm, 1), row),
                  pl.BlockSpec((1, LANES), fixed)],
        out_specs=[pl.BlockSpec((tm, ATTN_WIDTH), row),
                   pl.BlockSpec((tm, 2 * KV_WIDTH), row),
                   pl.BlockSpec((tm, 2 * KV_WIDTH), row),
                   pl.BlockSpec((tm, CONV_WIDTH), row)],
        out_shape=[jax.ShapeDtypeStruct((t, ATTN_WIDTH), BF16),
                   jax.ShapeDtypeStruct((t, 2 * KV_WIDTH), BF16),
                   jax.ShapeDtypeStruct((t, 2 * KV_WIDTH), BF16),
                   jax.ShapeDtypeStruct((t, CONV_WIDTH), F32)],
        compiler_params=_compiler_params(("parallel",)),
        name="in_proj",
    )(x2, w_cat, pos, invf)


def _attn_kernel(sink_ref, q_ref, k_ref, v_ref, o_ref):
    s_len = q_ref.shape[1]
    nb = s_len // BLOCK
    qi = lax.broadcasted_iota(jnp.int32, (BLOCK, 2 * BLOCK), 0)
    si = lax.broadcasted_iota(jnp.int32, (BLOCK, 2 * BLOCK), 1)
    band = jnp.logical_and(si > qi, si <= qi + WINDOW)
    lane = lax.broadcasted_iota(jnp.int32, (2 * BLOCK, LANES), 1)
    low = lane < HEAD_DIM
    heads_per_group = N_Q_HEADS // N_KV_HEADS
    zero = jnp.zeros((), BF16)

    def body(n, carry):
        start = pl.multiple_of(n * BLOCK, BLOCK)
        prev = pl.multiple_of(jnp.maximum(n - 1, 0) * BLOCK, BLOCK)
        valid = jnp.logical_and(band, si >= jnp.where(n == 0, BLOCK, 0))
        q = q_ref[0, pl.ds(start, BLOCK), :]
        k2 = jnp.concatenate([k_ref[0, pl.ds(prev, BLOCK), :],
                              k_ref[0, pl.ds(start, BLOCK), :]], axis=0)
        v2 = jnp.concatenate([v_ref[0, pl.ds(prev, BLOCK), :],
                              v_ref[0, pl.ds(start, BLOCK), :]], axis=0)
        for pair in range(N_Q_HEADS // 2):
            g = (2 * pair) // heads_per_group
            qp = q[:, pair * LANES:(pair + 1) * LANES]
            kg = k2[:, g * LANES:(g + 1) * LANES]
            vg = v2[:, g * LANES:(g + 1) * LANES]
            out = jnp.zeros((BLOCK, LANES), F32)
            for half in range(2):
                keep = low if half == 0 else jnp.logical_not(low)
                kx = jnp.where(keep, kg, zero)
                vx = jnp.where(keep, vg, zero)
                sink = sink_ref[2 * pair + half]
                s = lax.dot_general(qp, kx, (((1,), (1,)), ((), ())),
                                    preferred_element_type=F32)
                s = jnp.where(valid, s, NEG_INF)
                m = jnp.maximum(jnp.max(s, axis=-1, keepdims=True), sink)
                e = jnp.exp(s - m)
                denom = jnp.sum(e, axis=-1, keepdims=True) + jnp.exp(sink - m)
                pv = jnp.dot(e.astype(BF16), vx, preferred_element_type=F32)
                out = out + pv * (1.0 / denom)
            o_ref[0, pl.ds(start, BLOCK), pair * LANES:(pair + 1) * LANES] = out.astype(BF16)
        return carry

    lax.fori_loop(0, nb, body, 0)


def _attention(sinks, q3, k3, v3):
    b, s, _ = q3.shape
    blk = lambda w: pl.BlockSpec((1, s, w), lambda i: (i, 0, 0))
    return pl.pallas_call(
        _attn_kernel,
        grid=(b,),
        in_specs=[pl.BlockSpec(memory_space=pltpu.SMEM),
                  blk(ATTN_WIDTH), blk(2 * KV_WIDTH), blk(2 * KV_WIDTH)],
        out_specs=blk(ATTN_WIDTH),
        out_shape=jax.ShapeDtypeStruct((b, s, ATTN_WIDTH), BF16),
        compiler_params=_compiler_params(("parallel",)),
        name="attention",
    )(sinks, q3, k3, v3)


CONV_PAD = 32
CONV_ROWS = 128
CONV_COPY_ROWS = 256


def _conv_kernel(h_ref, w_ref, b_ref, g_ref, beta_ref, o_ref, sh_ref, y_ref):
    s_len = h_ref.shape[1]
    padded = CONV_PAD + s_len
    for l in range(CONV_WIDTH // LANES):
        cols = slice(l * LANES, (l + 1) * LANES)
        sh_ref[0, 0:CONV_PAD, :] = jnp.zeros((CONV_PAD, LANES), F32)
        sh_ref[0, CONV_PAD:, :] = h_ref[0, :, cols]
        for r in range(1, SUBLANES):
            for lo in range(SUBLANES, padded, CONV_COPY_ROWS):
                hi = min(lo + CONV_COPY_ROWS, padded)
                sh_ref[r, lo:hi, :] = sh_ref[0, lo - r:hi - r, :]

        def body(c, carry):
            base = pl.multiple_of(CONV_PAD + c * CONV_ROWS, SUBLANES)
            acc = jnp.zeros((CONV_ROWS, LANES), F32) + b_ref[:, cols]
            for j in range(CONV_TAPS):
                a, r = divmod(CONV_TAPS - 1 - j, SUBLANES)
                acc = acc + sh_ref[r, pl.ds(base - SUBLANES * a, CONV_ROWS), :] * w_ref[j:j + 1, cols]
            y_ref[pl.ds(pl.multiple_of(c * CONV_ROWS, SUBLANES), CONV_ROWS), cols] = acc
            return carry

        lax.fori_loop(0, s_len // CONV_ROWS, body, 0)

    def norm(c, carry):
        rows = pl.ds(pl.multiple_of(c * CONV_ROWS, SUBLANES), CONV_ROWS)
        y = _layer_norm(y_ref[rows, :], g_ref[...], beta_ref[...])
        o_ref[0, rows, :] = (y * jax.nn.sigmoid(y)).astype(BF16)
        return carry

    lax.fori_loop(0, s_len // CONV_ROWS, norm, 0)


def _conv_group(h3, conv_w, conv_b, ln_g, ln_b):
    b, s, _ = h3.shape
    blk = pl.BlockSpec((1, s, CONV_WIDTH), lambda i: (i, 0, 0))
    vec = pl.BlockSpec((1, CONV_WIDTH), lambda i: (0, 0))
    return pl.pallas_call(
        _conv_kernel,
        grid=(b,),
        in_specs=[blk, pl.BlockSpec((CONV_TAPS, CONV_WIDTH), lambda i: (0, 0)), vec, vec, vec],
        out_specs=blk,
        out_shape=jax.ShapeDtypeStruct((b, s, CONV_WIDTH), BF16),
        scratch_shapes=[pltpu.VMEM((SUBLANES, CONV_PAD + s, LANES), F32),
                        pltpu.VMEM((s, CONV_WIDTH), F32)],
        compiler_params=_compiler_params(("parallel",)),
        name="conv_group",
    )(h3, conv_w, conv_b, ln_g, ln_b)


def _outproj_kernel(att_ref, cnv_ref, x_ref, wa_ref, wc_ref, g_ref, b_ref, y_ref, yb_ref, ybt_ref):
    mixed = jnp.dot(att_ref[...], wa_ref[...], preferred_element_type=F32)
    mixed = mixed + jnp.dot(cnv_ref[...], wc_ref[...], preferred_element_type=F32)
    y = _layer_norm(DEEPNORM_ALPHA * x_ref[...] + mixed, g_ref[...], b_ref[...])
    y_ref[...] = y
    yb_ref[...] = y.astype(BF16)
    ybt_ref[...] = y.T.astype(BF16)


def _out_proj(att, cnv, x2, wa, wc, g, b, tm):
    t = x2.shape[0]
    row = lambda i: (i, 0)
    fixed = lambda i: (0, 0)
    return pl.pallas_call(
        _outproj_kernel,
        grid=(t // tm,),
        in_specs=[pl.BlockSpec((tm, ATTN_WIDTH), row), pl.BlockSpec((tm, CONV_WIDTH), row),
                  pl.BlockSpec((tm, D_MODEL), row),
                  pl.BlockSpec((ATTN_WIDTH, D_MODEL), fixed),
                  pl.BlockSpec((CONV_WIDTH, D_MODEL), fixed),
                  pl.BlockSpec((1, D_MODEL), fixed), pl.BlockSpec((1, D_MODEL), fixed)],
        out_specs=[pl.BlockSpec((tm, D_MODEL), row), pl.BlockSpec((tm, D_MODEL), row),
                   pl.BlockSpec((D_MODEL, tm), lambda i: (0, i))],
        out_shape=[jax.ShapeDtypeStruct((t, D_MODEL), F32),
                   jax.ShapeDtypeStruct((t, D_MODEL), BF16),
                   jax.ShapeDtypeStruct((D_MODEL, t), BF16)],
        compiler_params=_compiler_params(("parallel",)),
        name="out_proj",
    )(att, cnv, x2, wa, wc, g, b)


N_GROUPS = 2 * PEER_HEADS
KEY_VREGS = N_KEYS // SUBLANES


def _tree_argmax(vals, bases):
    items = [(v, float(b)) for v, b in zip(vals, bases)]
    while len(items) > 1:
        nxt = []
        for i in range(0, len(items) - 1, 2):
            (va, ba), (vb, bb) = items[i], items[i + 1]
            better = vb > va
            nxt.append((jnp.where(better, vb, va), jnp.where(better, bb, ba)))
        if len(items) % 2:
            nxt.append(items[-1])
        items = nxt
    return items[0]


def _tree_max(vals):
    items = list(vals)
    while len(items) > 1:
        nxt = [jnp.maximum(items[i], items[i + 1]) for i in range(0, len(items) - 1, 2)]
        if len(items) % 2:
            nxt.append(items[-1])
        items = nxt
    return items[0]


def _extract_topk(vals, bases, subf, rounds, break_ties):
    vals = list(vals)
    tops = []
    for a in range(rounds):
        code = RANK_CODE_BASE - a * RANK_CODE_STEP
        if break_ties:
            best, best_base = _tree_argmax(vals, bases)
            top = jnp.max(best, axis=0, keepdims=True)
            order = jnp.where(best == top, best_base + subf, 1.0e9)
            winner = jnp.min(order, axis=0, keepdims=True)
            winner_base = winner - subf
            vals = [jnp.where(winner_base == float(bases[k]), code, vals[k])
                    for k in range(len(vals))]
        else:
            top = jnp.max(_tree_max(vals), axis=0, keepdims=True)
            vals = [jnp.where(v == top, code, v) for v in vals]
        tops.append(top)
    return vals, tops


def _tie_flag(marked, rounds):
    count = jnp.zeros(marked[0].shape, F32)
    for m in marked:
        count = count + jnp.where(m < RANK_CODE_LIMIT, 1.0, 0.0)
    return jnp.where(jnp.sum(count, axis=0, keepdims=True) == float(rounds), 0.0, 1.0)


PAIR_VREGS = ((0, 0, 8, 0), (0, 8, 8, 8), (1, 0, 8, 16), (2, 0, 5, 32), (3, 0, 4, 48),
              (4, 0, 3, 64), (5, 0, 2, 80), (6, 0, 2, 96), (7, 0, 2, 112))
PAIR_TAIL_BASE = 128


def _routing_kernel(x_ref, wq_ref, keys_ref, e1_ref, c1_ref, e2_ref, r2_ref,
                    qb_ref, sc_ref, mk_ref, top_ref):
    tt = x_ref.shape[0]
    n_chunks = tt // LANES
    qb_ref[...] = jnp.dot(x_ref[...], wq_ref[...], preferred_element_type=F32).astype(BF16)
    for g in range(N_GROUPS):
        sc_ref[g] = lax.dot_general(keys_ref[g], qb_ref[:, g * PEER_HALF:(g + 1) * PEER_HALF],
                                    (((1,), (1,)), ((), ())), preferred_element_type=F32)

    sub = lax.broadcasted_iota(jnp.int32, (SUBLANES, LANES), 0)
    subf = sub.astype(F32)
    key_bases = tuple(SUBLANES * k for k in range(KEY_VREGS))

    pair_bases = tuple(v[3] for v in PAIR_VREGS) + (PAIR_TAIL_BASE,)
    no_ties = jnp.zeros((1, LANES), F32)

    def level1(g, tie, break_ties):
        for c in range(n_chunks):
            cols = slice(c * LANES, (c + 1) * LANES)
            vals = [sc_ref[g, SUBLANES * k:SUBLANES * (k + 1), cols] for k in range(KEY_VREGS)]
            marked, tops = _extract_topk(vals, key_bases, subf, PEER_TOPK, break_ties)
            if not break_ties:
                tie = jnp.maximum(tie, _tie_flag(marked, PEER_TOPK))
            for k in range(KEY_VREGS):
                mk_ref[g, SUBLANES * k:SUBLANES * (k + 1), cols] = marked[k]
            for a in range(PEER_TOPK):
                top_ref[g, a:a + 1, cols] = tops[a]
        return tie

    def level2(h, tie, break_ties):
        g1 = 2 * h
        g2 = 2 * h + 1
        for c in range(n_chunks):
            cols = slice(c * LANES, (c + 1) * LANES)
            v2 = (top_ref[g2, 0:SUBLANES, cols], top_ref[g2, SUBLANES:2 * SUBLANES, cols])
            m1 = top_ref[g1, 0:1, cols]
            m2 = top_ref[g2, 0:1, cols]
            cands = []
            for a, b0, used, _ in PAIR_VREGS:
                cand = top_ref[g1, a:a + 1, cols] + v2[b0 // SUBLANES]
                cands.append(cand if used == SUBLANES else jnp.where(sub < used, cand, PAD_SCORE))
            cands.append(top_ref[g1, SUBLANES:2 * SUBLANES, cols] + m2)
            marked, _ = _extract_topk(cands, pair_bases, subf, PEER_TOPK, break_ties)
            if not break_ties:
                tie = jnp.maximum(tie, _tie_flag(marked, PEER_TOPK))
            best = m1 + m2
            picked = [m < RANK_CODE_LIMIT for m in marked]
            ones = [jnp.where(p, 1.0, 0.0) for p in picked]
            zsum = jnp.zeros((SUBLANES, LANES), F32)
            for p, cand in zip(picked, cands):
                zsum = zsum + jnp.where(p, jnp.exp(cand - best), 0.0)
            inv_z = 0.5 / jnp.sum(zsum, axis=0, keepdims=True)
            widths = [jnp.sum(ones[0] + ones[1], axis=0, keepdims=True)]
            widths += [jnp.sum(ones[j], axis=0, keepdims=True) for j in range(2, len(PAIR_VREGS))]
            widths += [ones[-1][s:s + 1, :] for s in range(SUBLANES)]
            widths = [jnp.broadcast_to(w, (SUBLANES, LANES)) for w in widths]
            for kk in range(KEY_VREGS // 2):
                pair_rows = slice(2 * SUBLANES * kk, 2 * SUBLANES * (kk + 1))
                e2_pair, r2_pair = [], []
                for k in (2 * kk, 2 * kk + 1):
                    rows = slice(SUBLANES * k, SUBLANES * (k + 1))
                    mk1 = mk_ref[g1, rows, cols]
                    cnt1 = jnp.zeros((SUBLANES, LANES), F32)
                    for a in range(PEER_TOPK):
                        cnt1 = jnp.where(mk1 == RANK_CODE_BASE - a * RANK_CODE_STEP, widths[a], cnt1)
                    c1_ref[h, rows, cols] = cnt1
                    e1_ref[h, rows, cols] = jnp.exp(sc_ref[g1, rows, cols] - m1) * inv_z
                    mk2 = mk_ref[g2, rows, cols]
                    rank2 = jnp.floor((RANK_CODE_BASE - mk2) * (1.0 / RANK_CODE_STEP) + 0.5)
                    r2_pair.append(jnp.where(mk2 < RANK_CODE_LIMIT, rank2, float(PEER_TOPK)))
                    e2_pair.append(jnp.exp(sc_ref[g2, rows, cols] - m2))
                r2_ref[h, pair_rows, cols] = jnp.concatenate(r2_pair, axis=0).astype(BF16)
                e2_ref[h, pair_rows, cols] = jnp.concatenate(e2_pair, axis=0).astype(BF16)
        return tie

    tie = lax.fori_loop(0, N_GROUPS, lambda g, t: level1(g, t, False), no_ties)
    tie = lax.fori_loop(0, PEER_HEADS, lambda h, t: level2(h, t, False), tie)

    @pl.when(jnp.max(tie) > 0.0)
    def _():
        lax.fori_loop(0, N_GROUPS, lambda g, t: level1(g, t, True), no_ties)
        lax.fori_loop(0, PEER_HEADS, lambda h, t: level2(h, t, True), no_ties)


def _peer_routing(x1b, wq, keys, tt):
    t = x1b.shape[0]
    out_blk = pl.BlockSpec((PEER_HEADS, N_KEYS, tt), lambda i: (0, 0, i))
    sds = lambda dt: jax.ShapeDtypeStruct((PEER_HEADS, N_KEYS, t), dt)
    return pl.pallas_call(
        _routing_kernel,
        grid=(t // tt,),
        in_specs=[pl.BlockSpec((tt, D_MODEL), lambda i: (i, 0)),
                  pl.BlockSpec((D_MODEL, N_GROUPS * PEER_HALF), lambda i: (0, 0)),
                  pl.BlockSpec((N_GROUPS, N_KEYS, PEER_HALF), lambda i: (0, 0, 0))],
        out_specs=[out_blk] * 4,
        out_shape=[sds(F32), sds(F32), sds(BF16), sds(BF16)],
        scratch_shapes=[pltpu.VMEM((tt, N_GROUPS * PEER_HALF), BF16),
                        pltpu.VMEM((N_GROUPS, N_KEYS, tt), F32),
                        pltpu.VMEM((N_GROUPS, N_KEYS, tt), F32),
                        pltpu.VMEM((N_GROUPS, PEER_TOPK, tt), F32)],
        compiler_params=_compiler_params(("parallel",)),
        name="peer_routing",
    )(x1b, wq, keys)


BF16_ROWS = 2 * SUBLANES


DENSE_SUB = 1024


def _dense_kernel(xt_ref, u_ref, vt_ref, e1_ref, c1_ref, e2_in_ref, r2_in_ref, o_ref,
                  acc_ref, ht_ref, act_ref, e2_ref, r2_ref):
    j = pl.program_id(1)
    last = pl.num_programs(1) - 1
    _, eb, tt = ht_ref.shape
    n_sub = eb // DENSE_SUB
    zero = jnp.zeros((), BF16)
    write_slot = j % 2
    read_slot = 1 - write_slot

    def sub_rows(sb):
        return slice(sb * DENSE_SUB, (sb + 1) * DENSE_SUB)

    def stage_a(sb):
        ht_ref[write_slot, sub_rows(sb), :] = jnp.dot(u_ref[sub_rows(sb), :], xt_ref[...],
                                                      preferred_element_type=F32)

    def stage_b(sb):
        for il in range(sb * DENSE_SUB // N_KEYS, (sb + 1) * DENSE_SUB // N_KEYS):
            for tc in range(tt // LANES):
                cols = slice(tc * LANES, (tc + 1) * LANES)
                c1 = [jnp.broadcast_to(c1_ref[h, il:il + 1, cols], (BF16_ROWS, LANES)).astype(BF16)
                      for h in range(PEER_HEADS)]
                e1 = [jnp.broadcast_to(e1_ref[h, il:il + 1, cols], (BF16_ROWS, LANES)).astype(BF16)
                      for h in range(PEER_HEADS)]
                for p in range(N_KEYS // BF16_ROWS):
                    keys = slice(p * BF16_ROWS, (p + 1) * BF16_ROWS)
                    rows = slice(il * N_KEYS + p * BF16_ROWS, il * N_KEYS + (p + 1) * BF16_ROWS)
                    gate = jnp.zeros((BF16_ROWS, LANES), BF16)
                    for h in range(PEER_HEADS):
                        picked = jnp.where(r2_ref[h, keys, cols] < c1[h], e2_ref[h, keys, cols], zero)
                        gate = gate + picked * e1[h]
                    hh = ht_ref[read_slot, rows, cols]
                    gelu2 = hh * (1.0 + lax.erf(hh * (2.0 ** -0.5)))
                    act_ref[rows, cols] = gelu2.astype(BF16) * gate
        acc_ref[...] += jnp.dot(vt_ref[:, sub_rows(sb)], act_ref[sub_rows(sb), :],
                                preferred_element_type=F32)

    @pl.when(j == 0)
    def _():
        acc_ref[...] = jnp.zeros_like(acc_ref)
        e2_ref[...] = e2_in_ref[...]
        r2_ref[...] = r2_in_ref[...]
        for sb in range(n_sub):
            stage_a(sb)

    @pl.when(jnp.logical_and(j > 0, j < last))
    def _():
        stage_a(0)
        for sb in range(n_sub):
            if sb + 1 < n_sub:
                stage_a(sb + 1)
            stage_b(sb)

    @pl.when(j == last)
    def _():
        for sb in range(n_sub):
            stage_b(sb)
        o_ref[...] = acc_ref[...].T


def _peer_dense(x1bt, u, vt, e1, c1, e2, r2, tt, eb):
    t = x1bt.shape[1]
    n_blocks = N_EXPERTS // eb
    route = pl.BlockSpec((PEER_HEADS, N_KEYS, tt), lambda i, j: (0, 0, i))
    route_rows = pl.BlockSpec((PEER_HEADS, eb // N_KEYS, tt),
                              lambda i, j: (0, jnp.maximum(j - 1, 0), i))
    return pl.pallas_call(
        _dense_kernel,
        grid=(t // tt, n_blocks + 1),
        in_specs=[pl.BlockSpec((D_MODEL, tt), lambda i, j: (0, i)),
                  pl.BlockSpec((eb, D_MODEL), lambda i, j: (jnp.minimum(j, n_blocks - 1), 0)),
                  pl.BlockSpec((D_MODEL, eb), lambda i, j: (0, jnp.maximum(j - 1, 0))),
                  route_rows, route_rows, route, route],
        out_specs=pl.BlockSpec((tt, D_MODEL), lambda i, j: (i, 0)),
        out_shape=jax.ShapeDtypeStruct((t, D_MODEL), F32),
        scratch_shapes=[pltpu.VMEM((D_MODEL, tt), F32),
                        pltpu.VMEM((2, eb, tt), F32),
                        pltpu.VMEM((eb, tt), BF16),
                        pltpu.VMEM((PEER_HEADS, N_KEYS, tt), BF16),
                        pltpu.VMEM((PEER_HEADS, N_KEYS, tt), BF16)],
        compiler_params=_compiler_params(("parallel", "arbitrary")),
        name="peer_dense",
    )(x1bt, u, vt, e1, c1, e2, r2)


def _final_kernel(x1_ref, peer_ref, p_ref, wg_ref, wp_ref, g_ref, b_ref, o_ref):
    r = DEEPNORM_ALPHA * x1_ref[...] + peer_ref[...]
    gate = jax.nn.sigmoid(jnp.dot(r.astype(BF16), wg_ref[...], preferred_element_type=F32))
    emb = jnp.dot(p_ref[...].astype(BF16), wp_ref[...], preferred_element_type=F32)
    o_ref[...] = _layer_norm(r + gate * emb, g_ref[...], b_ref[...])


def _final(x1, peer, p2, wg, wp, g, b, tm):
    t = x1.shape[0]
    row = lambda i: (i, 0)
    fixed = lambda i: (0, 0)
    return pl.pallas_call(
        _final_kernel,
        grid=(t // tm,),
        in_specs=[pl.BlockSpec((tm, D_MODEL), row), pl.BlockSpec((tm, D_MODEL), row),
                  pl.BlockSpec((tm, PLE_DIM), row),
                  pl.BlockSpec((D_MODEL, D_MODEL), fixed), pl.BlockSpec((PLE_DIM, D_MODEL), fixed),
                  pl.BlockSpec((1, D_MODEL), fixed), pl.BlockSpec((1, D_MODEL), fixed)],
        out_specs=pl.BlockSpec((tm, D_MODEL), row),
        out_shape=jax.ShapeDtypeStruct((t, D_MODEL), F32),
        compiler_params=_compiler_params(("parallel",)),
        name="final",
    )(x1, peer, p2, wg, wp, g, b)


ROW_TILE = 512
ROUTE_TILE = 256
DENSE_TOKENS = 512
DENSE_EXPERTS = 2048


def _rope_lane_table():
    inv_freq = ROPE_THETA ** (-jnp.arange(ROPE_HALF, dtype=F32) * (2.0 / ROPE_DIM))
    lane = jnp.arange(LANES) % HEAD_DIM
    return jnp.where(lane < ROPE_DIM, inv_freq[lane % ROPE_HALF], 0.0).reshape(1, LANES).astype(F32)


def _dup_kv(w):
    h0, h1 = w[:, :HEAD_DIM], w[:, HEAD_DIM:]
    return jnp.concatenate([h0, h0, h1, h1], axis=1)


def kernel(x, p, positions, w_in, sinks, conv_w, conv_b, conv_ln_g, conv_ln_b, w_out, ln1_g, ln1_b,
           peer_query, peer_keys, peer_u, peer_v, ple_proj, ple_gate, ln2_g, ln2_b):
    b, s, d = x.shape
    t = b * s
    pos = positions.reshape(t, 1).astype(F32)
    invf = _rope_lane_table()
    xc = x.reshape(t, d)
    vec = lambda a: a.reshape(1, -1)
    for i in range(DEPTH):
        w = w_in[i]
        o = ATTN_WIDTH
        w_cat = jnp.concatenate(
            [w[:, :o], _dup_kv(w[:, o:o + KV_WIDTH]), _dup_kv(w[:, o + KV_WIDTH:o + 2 * KV_WIDTH]),
             w[:, o + 2 * KV_WIDTH:]], axis=1).astype(BF16)
        q, k, v, glu = _in_proj(xc, w_cat, pos, invf, ROW_TILE)
        att = _attention(sinks[i], q.reshape(b, s, -1), k.reshape(b, s, -1), v.reshape(b, s, -1))
        cnv = _conv_group(glu.reshape(b, s, -1), conv_w[i], vec(conv_b[i]),
                          vec(conv_ln_g[i]), vec(conv_ln_b[i]))
        wo = w_out[i].astype(BF16)
        x1, x1b, x1bt = _out_proj(att.reshape(t, -1), cnv.reshape(t, -1), xc, wo[:ATTN_WIDTH],
                            wo[ATTN_WIDTH:], vec(ln1_g[i]), vec(ln1_b[i]), ROW_TILE)
        keys = peer_keys[i].reshape(N_GROUPS, N_KEYS, PEER_HALF).astype(BF16)
        e1, c1, e2, r2 = _peer_routing(x1b, peer_query[i].astype(BF16), keys, ROUTE_TILE)
        peer = _peer_dense(x1bt, peer_u[i].astype(BF16), peer_v[i].astype(BF16).T, e1, c1, e2, r2,
                           DENSE_TOKENS, DENSE_EXPERTS)
        xc = _final(x1, peer, p[i].reshape(t, -1), ple_gate[i].astype(BF16),
                    ple_proj[i].astype(BF16), vec(ln2_g[i]), vec(ln2_b[i]), ROW_TILE)
    return xc.reshape(b, s, d)
```

```python
import jax
import jax.numpy as jnp
from jax import lax
from jax.experimental import pallas as pl
from jax.experimental.pallas import tpu as pltpu

F32 = jnp.float32
BF16 = jnp.bfloat16

D_MODEL = 1024
N_Q_HEADS = 8
N_KV_HEADS = 2
HEAD_DIM = 64
ATTN_WIDTH = N_Q_HEADS * HEAD_DIM
KV_WIDTH = N_KV_HEADS * HEAD_DIM
WINDOW = 128
BLOCK = 128
ROPE_THETA = 500000.0
ROPE_DIM = HEAD_DIM // 4
ROPE_HALF = ROPE_DIM // 2
CONV_WIDTH = D_MODEL - ATTN_WIDTH
CONV_TAPS = 31
PEER_HEADS = 8
N_KEYS = 128
N_EXPERTS = N_KEYS * N_KEYS
PEER_HALF = 128
PEER_TOPK = 16
PLE_DIM = 256
DEPTH = 1
DEEPNORM_ALPHA = (2 * DEPTH) ** 0.25
LN_EPS = 1e-5
NEG_INF = -1e30

LANES = 128
SUBLANES = 8
VMEM_LIMIT_BYTES = 56 * 1024 * 1024

RANK_CODE_BASE = -(2.0 ** 127)
RANK_CODE_STEP = 2.0 ** 120
RANK_CODE_LIMIT = -1.5 * 2.0 ** 126
PAD_SCORE = -(2.0 ** 126)


def _compiler_params(semantics, flags=None):
    return pltpu.CompilerParams(dimension_semantics=semantics,
                                vmem_limit_bytes=VMEM_LIMIT_BYTES, flags=flags)


def _layer_norm(v, g, b):
    mu = jnp.mean(v, axis=-1, keepdims=True)
    d = v - mu
    var = jnp.mean(d * d, axis=-1, keepdims=True)
    return d * lax.rsqrt(var + LN_EPS) * g + b


QK_GROUPS = (ATTN_WIDTH + 2 * KV_WIDTH) // LANES


def _inproj_kernel(x_ref, w_ref, pos_ref, invf_ref, q_ref, k_ref, v_ref, glu_ref):
    h = jnp.dot(x_ref[...].astype(BF16), w_ref[...], preferred_element_type=F32)
    tm = h.shape[0]
    ang = pos_ref[...] * invf_ref[...]
    lane = lax.broadcasted_iota(jnp.int32, (tm, LANES), 1) % HEAD_DIM
    first = lane < ROPE_HALF
    second = jnp.logical_and(lane >= ROPE_HALF, lane < ROPE_DIM)
    cos = jnp.cos(ang)
    sin = jnp.sin(ang)
    sin_signed = jnp.where(first, -sin, jnp.where(second, sin, 0.0))

    def rotate(t):
        up = pltpu.roll(t, LANES - ROPE_HALF, 1)
        dn = pltpu.roll(t, ROPE_HALF, 1)
        return t * cos + jnp.where(first, up, dn) * sin_signed

    scale = HEAD_DIM ** -0.5
    for g in range(ATTN_WIDTH // LANES):
        q_ref[:, g * LANES:(g + 1) * LANES] = (
            rotate(h[:, g * LANES:(g + 1) * LANES]) * scale).astype(BF16)
    o = ATTN_WIDTH
    for g in range(2 * KV_WIDTH // LANES):
        k_ref[:, g * LANES:(g + 1) * LANES] = rotate(
            h[:, o + g * LANES:o + (g + 1) * LANES]).astype(BF16)
    o += 2 * KV_WIDTH
    v_ref[...] = h[:, o:o + 2 * KV_WIDTH].astype(BF16)
    o += 2 * KV_WIDTH
    glu_ref[...] = h[:, o:o + CONV_WIDTH] * jax.nn.sigmoid(
        h[:, o + CONV_WIDTH:o + 2 * CONV_WIDTH])


def _in_proj(x2, w_cat, pos, invf, tm):
    t = x2.shape[0]
    n_in = w_cat.shape[1]
    row = lambda i: (i, 0)
    fixed = lambda i: (0, 0)
    return pl.pallas_call(
        _inproj_kernel,
        grid=(t // tm,),
        in_specs=[pl.BlockSpec((tm, D_MODEL), row),
                  pl.BlockSpec((D_MODEL, n_in), fixed),
                  pl.BlockSpec((tm, 1), row),
                  pl.BlockSpec((1, LANES), fixed)],
        out_specs=[pl.BlockSpec((tm, ATTN_WIDTH), row),
                   pl.BlockSpec((tm, 2 * KV_WIDTH), row),
                   pl.BlockSpec((tm, 2 * KV_WIDTH), row),
                   pl.BlockSpec((tm, CONV_WIDTH), row)],
        out_shape=[jax.ShapeDtypeStruct((t, ATTN_WIDTH), BF16),
                   jax.ShapeDtypeStruct((t, 2 * KV_WIDTH), BF16),
                   jax.ShapeDtypeStruct((t, 2 * KV_WIDTH), BF16),
                   jax.ShapeDtypeStruct((t, CONV_WIDTH), F32)],
        compiler_params=_compiler_params(("parallel",)),
        name="in_proj",
    )(x2, w_cat, pos, invf)


def _attn_kernel(sink_ref, q_ref, k_ref, v_ref, o_ref):
    s_len = q_ref.shape[1]
    nb = s_len // BLOCK
    qi = lax.broadcasted_iota(jnp.int32, (BLOCK, 2 * BLOCK), 0)
    si = lax.broadcasted_iota(jnp.int32, (BLOCK, 2 * BLOCK), 1)
    band = jnp.logical_and(si > qi, si <= qi + WINDOW)
    lane = lax.broadcasted_iota(jnp.int32, (2 * BLOCK, LANES), 1)
    low = lane < HEAD_DIM
    heads_per_group = N_Q_HEADS // N_KV_HEADS
    zero = jnp.zeros((), BF16)

    def body(n, carry):
        start = pl.multiple_of(n * BLOCK, BLOCK)
        prev = pl.multiple_of(jnp.maximum(n - 1, 0) * BLOCK, BLOCK)
        valid = jnp.logical_and(band, si >= jnp.where(n == 0, BLOCK, 0))
        q = q_ref[0, pl.ds(start, BLOCK), :]
        k2 = jnp.concatenate([k_ref[0, pl.ds(prev, BLOCK), :],
                              k_ref[0, pl.ds(start, BLOCK), :]], axis=0)
        v2 = jnp.concatenate([v_ref[0, pl.ds(prev, BLOCK), :],
                              v_ref[0, pl.ds(start, BLOCK), :]], axis=0)
        for pair in range(N_Q_HEADS // 2):
            g = (2 * pair) // heads_per_group
            qp = q[:, pair * LANES:(pair + 1) * LANES]
            kg = k2[:, g * LANES:(g + 1) * LANES]
            vg = v2[:, g * LANES:(g + 1) * LANES]
            out = jnp.zeros((BLOCK, LANES), F32)
            for half in range(2):
                keep = low if half == 0 else jnp.logical_not(low)
                kx = jnp.where(keep, kg, zero)
                vx = jnp.where(keep, vg, zero)
                sink = sink_ref[2 * pair + half]
                s = lax.dot_general(qp, kx, (((1,), (1,)), ((), ())),
                                    preferred_element_type=F32)
                s = jnp.where(valid, s, NEG_INF)
                m = jnp.maximum(jnp.max(s, axis=-1, keepdims=True), sink)
                e = jnp.exp(s - m)
                denom = jnp.sum(e, axis=-1, keepdims=True) + jnp.exp(sink - m)
                pv = jnp.dot(e.astype(BF16), vx, preferred_element_type=F32)
                out = out + pv * (1.0 / denom)
            o_ref[0, pl.ds(start, BLOCK), pair * LANES:(pair + 1) * LANES] = out.astype(BF16)
        return carry

    lax.fori_loop(0, nb, body, 0)


def _attention(sinks, q3, k3, v3):
    b, s, _ = q3.shape
    blk = lambda w: pl.BlockSpec((1, s, w), lambda i: (i, 0, 0))
    return pl.pallas_call(
        _attn_kernel,
        grid=(b,),
        in_specs=[pl.BlockSpec(memory_space=pltpu.SMEM),
                  blk(ATTN_WIDTH), blk(2 * KV_WIDTH), blk(2 * KV_WIDTH)],
        out_specs=blk(ATTN_WIDTH),
        out_shape=jax.ShapeDtypeStruct((b, s, ATTN_WIDTH), BF16),
        compiler_params=_compiler_params(("parallel",)),
        name="attention",
    )(sinks, q3, k3, v3)


CONV_PAD = 32
CONV_ROWS = 128
CONV_COPY_ROWS = 256


def _conv_kernel(h_ref, w_ref, b_ref, g_ref, beta_ref, o_ref, sh_ref, y_ref):
    s_len = h_ref.shape[1]
    padded = CONV_PAD + s_len
    for l in range(CONV_WIDTH // LANES):
        cols = slice(l * LANES, (l + 1) * LANES)
        sh_ref[0, 0:CONV_PAD, :] = jnp.zeros((CONV_PAD, LANES), F32)
        sh_ref[0, CONV_PAD:, :] = h_ref[0, :, cols]
        for r in range(1, SUBLANES):
            for lo in range(SUBLANES, padded, CONV_COPY_ROWS):
                hi = min(lo + CONV_COPY_ROWS, padded)
                sh_ref[r, lo:hi, :] = sh_ref[0, lo - r:hi - r, :]

        def body(c, carry):
            base = pl.multiple_of(CONV_PAD + c * CONV_ROWS, SUBLANES)
            acc = jnp.zeros((CONV_ROWS, LANES), F32) + b_ref[:, cols]
            for j in range(CONV_TAPS):
                a, r = divmod(CONV_TAPS - 1 - j, SUBLANES)
                acc = acc + sh_ref[r, pl.ds(base - SUBLANES * a, CONV_ROWS), :] * w_ref[j:j + 1, cols]
            y_ref[pl.ds(pl.multiple_of(c * CONV_ROWS, SUBLANES), CONV_ROWS), cols] = acc
            return carry

        lax.fori_loop(0, s_len // CONV_ROWS, body, 0)

    def norm(c, carry):
        rows = pl.ds(pl.multiple_of(c * CONV_ROWS, SUBLANES), CONV_ROWS)
        y = _layer_norm(y_ref[rows, :], g_ref[...], beta_ref[...])
        o_ref[0, rows, :] = (y * jax.nn.sigmoid(y)).astype(BF16)
        return carry

    lax.fori_loop(0, s_len // CONV_ROWS, norm, 0)


def _conv_group(h3, conv_w, conv_b, ln_g, ln_b):
    b, s, _ = h3.shape
    blk = pl.BlockSpec((1, s, CONV_WIDTH), lambda i: (i, 0, 0))
    vec = pl.BlockSpec((1, CONV_WIDTH), lambda i: (0, 0))
    return pl.pallas_call(
        _conv_kernel,
        grid=(b,),
        in_specs=[blk, pl.BlockSpec((CONV_TAPS, CONV_WIDTH), lambda i: (0, 0)), vec, vec, vec],
        out_specs=blk,
        out_shape=jax.ShapeDtypeStruct((b, s, CONV_WIDTH), BF16),
        scratch_shapes=[pltpu.VMEM((SUBLANES, CONV_PAD + s, LANES), F32),
                        pltpu.VMEM((s, CONV_WIDTH), F32)],
        compiler_params=_compiler_params(("parallel",)),
        name="conv_group",
    )(h3, conv_w, conv_b, ln_g, ln_b)


def _outproj_kernel(att_ref, cnv_ref, x_ref, wa_ref, wc_ref, g_ref, b_ref, y_ref, yb_ref, ybt_ref):
    mixed = jnp.dot(att_ref[...], wa_ref[...], preferred_element_type=F32)
    mixed = mixed + jnp.dot(cnv_ref[...], wc_ref[...], preferred_element_type=F32)
    y = _layer_norm(DEEPNORM_ALPHA * x_ref[...] + mixed, g_ref[...], b_ref[...])
    y_ref[...] = y
    yb_ref[...] = y.astype(BF16)
    ybt_ref[...] = y.T.astype(BF16)


def _out_proj(att, cnv, x2, wa, wc, g, b, tm):
    t = x2.shape[0]
    row = lambda i: (i, 0)
    fixed = lambda i: (0, 0)
    return pl.pallas_call(
        _outproj_kernel,
        grid=(t // tm,),
        in_specs=[pl.BlockSpec((tm, ATTN_WIDTH), row), pl.BlockSpec((tm, CONV_WIDTH), row),
                  pl.BlockSpec((tm, D_MODEL), row),
                  pl.BlockSpec((ATTN_WIDTH, D_MODEL), fixed),
                  pl.BlockSpec((CONV_WIDTH, D_MODEL), fixed),
                  pl.BlockSpec((1, D_MODEL), fixed), pl.BlockSpec((1, D_MODEL), fixed)],
        out_specs=[pl.BlockSpec((tm, D_MODEL), row), pl.BlockSpec((tm, D_MODEL), row),
                   pl.BlockSpec((D_MODEL, tm), lambda i: (0, i))],
        out_shape=[jax.ShapeDtypeStruct((t, D_MODEL), F32),
                   jax.ShapeDtypeStruct((t, D_MODEL), BF16),
                   jax.ShapeDtypeStruct((D_MODEL, t), BF16)],
        compiler_params=_compiler_params(("parallel",)),
        name="out_proj",
    )(att, cnv, x2, wa, wc, g, b)


N_GROUPS = 2 * PEER_HEADS
KEY_VREGS = N_KEYS // SUBLANES


def _tree_argmax(vals, bases):
    items = [(v, float(b)) for v, b in zip(vals, bases)]
    while len(items) > 1:
        nxt = []
        for i in range(0, len(items) - 1, 2):
            (va, ba), (vb, bb) = items[i], items[i + 1]
            better = vb > va
            nxt.append((jnp.where(better, vb, va), jnp.where(better, bb, ba)))
        if len(items) % 2:
            nxt.append(items[-1])
        items = nxt
    return items[0]


def _tree_max(vals):
    items = list(vals)
    while len(items) > 1:
        nxt = [jnp.maximum(items[i], items[i + 1]) for i in range(0, len(items) - 1, 2)]
        if len(items) % 2:
            nxt.append(items[-1])
        items = nxt
    return items[0]


def _extract_topk(vals, bases, subf, rounds, break_ties):
    vals = list(vals)
    tops = []
    for a in range(rounds):
        code = RANK_CODE_BASE - a * RANK_CODE_STEP
        if break_ties:
            best, best_base = _tree_argmax(vals, bases)
            top = jnp.max(best, axis=0, keepdims=True)
            order = jnp.where(best == top, best_base + subf, 1.0e9)
            winner = jnp.min(order, axis=0, keepdims=True)
            winner_base = winner - subf
            vals = [jnp.where(winner_base == float(bases[k]), code, vals[k])
                    for k in range(len(vals))]
        else:
            top = jnp.max(_tree_max(vals), axis=0, keepdims=True)
            vals = [jnp.where(v == top, code, v) for v in vals]
        tops.append(top)
    return vals, tops


def _tie_flag(marked, rounds):
    count = jnp.zeros(marked[0].shape, F32)
    for m in marked:
        count = count + jnp.where(m < RANK_CODE_LIMIT, 1.0, 0.0)
    return jnp.where(jnp.sum(count, axis=0, keepdims=True) == float(rounds), 0.0, 1.0)


PAIR_VREGS = ((0, 0, 8, 0), (0, 8, 8, 8), (1, 0, 8, 16), (2, 0, 5, 32), (3, 0, 4, 48),
              (4, 0, 3, 64), (5, 0, 2, 80), (6, 0, 2, 96), (7, 0, 2, 112))
PAIR_TAIL_BASE = 128


def _routing_kernel(x_ref, wq_ref, keys_ref, e1_ref, c1_ref, e2_ref, r2_ref,
                    qb_ref, sc_ref, mk_ref, top_ref):
    tt = x_ref.shape[0]
    n_chunks = tt // LANES
    qb_ref[...] = jnp.dot(x_ref[...], wq_ref[...], preferred_element_type=F32).astype(BF16)
    for g in range(N_GROUPS):
        sc_ref[g] = lax.dot_general(keys_ref[g], qb_ref[:, g * PEER_HALF:(g + 1) * PEER_HALF],
                                    (((1,), (1,)), ((), ())), preferred_element_type=F32)

    sub = lax.broadcasted_iota(jnp.int32, (SUBLANES, LANES), 0)
    subf = sub.astype(F32)
    key_bases = tuple(SUBLANES * k for k in range(KEY_VREGS))

    pair_bases = tuple(v[3] for v in PAIR_VREGS) + (PAIR_TAIL_BASE,)
    no_ties = jnp.zeros((1, LANES), F32)

    def level1(g, tie, break_ties):
        for c in range(n_chunks):
            cols = slice(c * LANES, (c + 1) * LANES)
            vals = [sc_ref[g, SUBLANES * k:SUBLANES * (k + 1), cols] for k in range(KEY_VREGS)]
            marked, tops = _extract_topk(vals, key_bases, subf, PEER_TOPK, break_ties)
            if not break_ties:
                tie = jnp.maximum(tie, _tie_flag(marked, PEER_TOPK))
            for k in range(KEY_VREGS):
                mk_ref[g, SUBLANES * k:SUBLANES * (k + 1), cols] = marked[k]
            for a in range(PEER_TOPK):
                top_ref[g, a:a + 1, cols] = tops[a]
        return tie

    def level2(h, tie, break_ties):
        g1 = 2 * h
        g2 = 2 * h + 1
        for c in range(n_chunks):
            cols = slice(c * LANES, (c + 1) * LANES)
            v2 = (top_ref[g2, 0:SUBLANES, cols], top_ref[g2, SUBLANES:2 * SUBLANES, cols])
            m1 = top_ref[g1, 0:1, cols]
            m2 = top_ref[g2, 0:1, cols]
            cands = []
            for a, b0, used, _ in PAIR_VREGS:
                cand = top_ref[g1, a:a + 1, cols] + v2[b0 // SUBLANES]
                cands.append(cand if used == SUBLANES else jnp.where(sub < used, cand, PAD_SCORE))
            cands.append(top_ref[g1, SUBLANES:2 * SUBLANES, cols] + m2)
            marked, _ = _extract_topk(cands, pair_bases, subf, PEER_TOPK, break_ties)
            if not break_ties:
                tie = jnp.maximum(tie, _tie_flag(marked, PEER_TOPK))
            best = m1 + m2
            picked = [m < RANK_CODE_LIMIT for m in marked]
            ones = [jnp.where(p, 1.0, 0.0) for p in picked]
            zsum = jnp.zeros((SUBLANES, LANES), F32)
            for p, cand in zip(picked, cands):
                zsum = zsum + jnp.where(p, jnp.exp(cand - best), 0.0)
            inv_z = 0.5 / jnp.sum(zsum, axis=0, keepdims=True)
            widths = [jnp.sum(ones[0] + ones[1], axis=0, keepdims=True)]
            widths += [jnp.sum(ones[j], axis=0, keepdims=True) for j in range(2, len(PAIR_VREGS))]
            widths += [ones[-1][s:s + 1, :] for s in range(SUBLANES)]
            widths = [jnp.broadcast_to(w, (SUBLANES, LANES)) for w in widths]
            for kk in range(KEY_VREGS // 2):
                pair_rows = slice(2 * SUBLANES * kk, 2 * SUBLANES * (kk + 1))
                e2_pair, r2_pair = [], []
                for k in (2 * kk, 2 * kk + 1):
                    rows = slice(SUBLANES * k, SUBLANES * (k + 1))
                    mk1 = mk_ref[g1, rows, cols]
                    cnt1 = jnp.zeros((SUBLANES, LANES), F32)
                    for a in range(PEER_TOPK):
                        cnt1 = jnp.where(mk1 == RANK_CODE_BASE - a * RANK_CODE_STEP, widths[a], cnt1)
                    c1_ref[h, rows, cols] = cnt1
                    e1_ref[h, rows, cols] = jnp.exp(sc_ref[g1, rows, cols] - m1) * inv_z
                    mk2 = mk_ref[g2, rows, cols]
                    rank2 = jnp.floor((RANK_CODE_BASE - mk2) * (1.0 / RANK_CODE_STEP) + 0.5)
                    r2_pair.append(jnp.where(mk2 < RANK_CODE_LIMIT, rank2, float(PEER_TOPK)))
                    e2_pair.append(jnp.exp(sc_ref[g2, rows, cols] - m2))
                r2_ref[h, pair_rows, cols] = jnp.concatenate(r2_pair, axis=0).astype(BF16)
                e2_ref[h, pair_rows, cols] = jnp.concatenate(e2_pair, axis=0).astype(BF16)
        return tie

    tie = lax.fori_loop(0, N_GROUPS, lambda g, t: level1(g, t, False), no_ties)
    tie = lax.fori_loop(0, PEER_HEADS, lambda h, t: level2(h, t, False), tie)

    @pl.when(jnp.max(tie) > 0.0)
    def _():
        lax.fori_loop(0, N_GROUPS, lambda g, t: level1(g, t, True), no_ties)
        lax.fori_loop(0, PEER_HEADS, lambda h, t: level2(h, t, True), no_ties)


def _peer_routing(x1b, wq, keys, tt):
    t = x1b.shape[0]
    out_blk = pl.BlockSpec((PEER_HEADS, N_KEYS, tt), lambda i: (0, 0, i))
    sds = lambda dt: jax.ShapeDtypeStruct((PEER_HEADS, N_KEYS, t), dt)
    return pl.pallas_call(
        _routing_kernel,
        grid=(t // tt,),
        in_specs=[pl.BlockSpec((tt, D_MODEL), lambda i: (i, 0)),
                  pl.BlockSpec((D_MODEL, N_GROUPS * PEER_HALF), lambda i: (0, 0)),
                  pl.BlockSpec((N_GROUPS, N_KEYS, PEER_HALF), lambda i: (0, 0, 0))],
        out_specs=[out_blk] * 4,
        out_shape=[sds(F32), sds(F32), sds(BF16), sds(BF16)],
        scratch_shapes=[pltpu.VMEM((tt, N_GROUPS * PEER_HALF), BF16),
                        pltpu.VMEM((N_GROUPS, N_KEYS, tt), F32),
                        pltpu.VMEM((N_GROUPS, N_KEYS, tt), F32),
                        pltpu.VMEM((N_GROUPS, PEER_TOPK, tt), F32)],
        compiler_params=_compiler_params(("parallel",)),
        name="peer_routing",
    )(x1b, wq, keys)


BF16_ROWS = 2 * SUBLANES


DENSE_SUB = 1024


def _dense_kernel(xt_ref, u_ref, vt_ref, e1_ref, c1_ref, e2_in_ref, r2_in_ref, o_ref,
                  acc_ref, ht_ref, act_ref, e2_ref, r2_ref):
    j = pl.program_id(1)
    last = pl.num_programs(1) - 1
    _, eb, tt = ht_ref.shape
    n_sub = eb // DENSE_SUB
    zero = jnp.zeros((), BF16)
    write_slot = j % 2
    read_slot = 1 - write_slot

    def sub_rows(sb):
        return slice(sb * DENSE_SUB, (sb + 1) * DENSE_SUB)

    def stage_a(sb):
        ht_ref[write_slot, sub_rows(sb), :] = jnp.dot(u_ref[sub_rows(sb), :], xt_ref[...],
                                                      preferred_element_type=F32)

    def stage_b(sb):
        for il in range(sb * DENSE_SUB // N_KEYS, (sb + 1) * DENSE_SUB // N_KEYS):
            for tc in range(tt // LANES):
                cols = slice(tc * LANES, (tc + 1) * LANES)
                c1 = [jnp.broadcast_to(c1_ref[h, il:il + 1, cols], (BF16_ROWS, LANES)).astype(BF16)
                      for h in range(PEER_HEADS)]
                e1 = [jnp.broadcast_to(e1_ref[h, il:il + 1, cols], (BF16_ROWS, LANES)).astype(BF16)
                      for h in range(PEER_HEADS)]
                for p in range(N_KEYS // BF16_ROWS):
                    keys = slice(p * BF16_ROWS, (p + 1) * BF16_ROWS)
                    rows = slice(il * N_KEYS + p * BF16_ROWS, il * N_KEYS + (p + 1) * BF16_ROWS)
                    gate = jnp.zeros((BF16_ROWS, LANES), BF16)
                    for h in range(PEER_HEADS):
                        picked = jnp.where(r2_ref[h, keys, cols] < c1[h], e2_ref[h, keys, cols], zero)
                        gate = gate + picked * e1[h]
                    hh = ht_ref[read_slot, rows, cols]
                    gelu2 = hh * (1.0 + lax.erf(hh * (2.0 ** -0.5)))
                    act_ref[rows, cols] = gelu2.astype(BF16) * gate
        acc_ref[...] += jnp.dot(vt_ref[:, sub_rows(sb)], act_ref[sub_rows(sb), :],
                                preferred_element_type=F32)

    @pl.when(j == 0)
    def _():
        acc_ref[...] = jnp.zeros_like(acc_ref)
        e2_ref[...] = e2_in_ref[...]
        r2_ref[...] = r2_in_ref[...]
        for sb in range(n_sub):
            stage_a(sb)

    @pl.when(jnp.logical_and(j > 0, j < last))
    def _():
        stage_a(0)
        for sb in range(n_sub):
            if sb + 1 < n_sub:
                stage_a(sb + 1)
            stage_b(sb)

    @pl.when(j == last)
    def _():
        for sb in range(n_sub):
            stage_b(sb)
        o_ref[...] = acc_ref[...].T


def _peer_dense(x1bt, u, vt, e1, c1, e2, r2, tt, eb):
    t = x1bt.shape[1]
    n_blocks = N_EXPERTS // eb
    route = pl.BlockSpec((PEER_HEADS, N_KEYS, tt), lambda i, j: (0, 0, i))
    route_rows = pl.BlockSpec((PEER_HEADS, eb // N_KEYS, tt),
                              lambda i, j: (0, jnp.maximum(j - 1, 0), i))
    return pl.pallas_call(
        _dense_kernel,
        grid=(t // tt, n_blocks + 1),
        in_specs=[pl.BlockSpec((D_MODEL, tt), lambda i, j: (0, i)),
                  pl.BlockSpec((eb, D_MODEL), lambda i, j: (jnp.minimum(j, n_blocks - 1), 0)),
                  pl.BlockSpec((D_MODEL, eb), lambda i, j: (0, jnp.maximum(j - 1, 0))),
                  route_rows, route_rows, route, route],
        out_specs=pl.BlockSpec((tt, D_MODEL), lambda i, j: (i, 0)),
        out_shape=jax.ShapeDtypeStruct((t, D_MODEL), F32),
        scratch_shapes=[pltpu.VMEM((D_MODEL, tt), F32),
                        pltpu.VMEM((2, eb, tt), F32),
                        pltpu.VMEM((eb, tt), BF16),
                        pltpu.VMEM((PEER_HEADS, N_KEYS, tt), BF16),
                        pltpu.VMEM((PEER_HEADS, N_KEYS, tt), BF16)],
        compiler_params=_compiler_params(("parallel", "arbitrary")),
        name="peer_dense",
    )(x1bt, u, vt, e1, c1, e2, r2)


def _final_kernel(x1_ref, peer_ref, p_ref, wg_ref, wp_ref, g_ref, b_ref, o_ref):
    r = DEEPNORM_ALPHA * x1_ref[...] + peer_ref[...]
    gate = jax.nn.sigmoid(jnp.dot(r.astype(BF16), wg_ref[...], preferred_element_type=F32))
    emb = jnp.dot(p_ref[...].astype(BF16), wp_ref[...], preferred_element_type=F32)
    o_ref[...] = _layer_norm(r + gate * emb, g_ref[...], b_ref[...])


def _final(x1, peer, p2, wg, wp, g, b, tm):
    t = x1.shape[0]
    row = lambda i: (i, 0)
    fixed = lambda i: (0, 0)
    return pl.pallas_call(
        _final_kernel,
        grid=(t // tm,),
        in_specs=[pl.BlockSpec((tm, D_MODEL), row), pl.BlockSpec((tm, D_MODEL), row),
                  pl.BlockSpec((tm, PLE_DIM), row),
                  pl.BlockSpec((D_MODEL, D_MODEL), fixed), pl.BlockSpec((PLE_DIM, D_MODEL), fixed),
                  pl.BlockSpec((1, D_MODEL), fixed), pl.BlockSpec((1, D_MODEL), fixed)],
        out_specs=pl.BlockSpec((tm, D_MODEL), row),
        out_shape=jax.ShapeDtypeStruct((t, D_MODEL), F32),
        compiler_params=_compiler_params(("parallel",)),
        name="final",
    )(x1, peer, p2, wg, wp, g, b)


ROW_TILE = 1024
ROUTE_TILE = 256
DENSE_TOKENS = 512
DENSE_EXPERTS = 2048


def _rope_lane_table():
    inv_freq = ROPE_THETA ** (-jnp.arange(ROPE_HALF, dtype=F32) * (2.0 / ROPE_DIM))
    lane = jnp.arange(LANES) % HEAD_DIM
    return jnp.where(lane < ROPE_DIM, inv_freq[lane % ROPE_HALF], 0.0).reshape(1, LANES).astype(F32)


def _dup_kv(w):
    h0, h1 = w[:, :HEAD_DIM], w[:, HEAD_DIM:]
    return jnp.concatenate([h0, h0, h1, h1], axis=1)


def kernel(x, p, positions, w_in, sinks, conv_w, conv_b, conv_ln_g, conv_ln_b, w_out, ln1_g, ln1_b,
           peer_query, peer_keys, peer_u, peer_v, ple_proj, ple_gate, ln2_g, ln2_b):
    b, s, d = x.shape
    t = b * s
    pos = positions.reshape(t, 1).astype(F32)
    invf = _rope_lane_table()
    xc = x.reshape(t, d)
    vec = lambda a: a.reshape(1, -1)
    for i in range(DEPTH):
        w = w_in[i]
        o = ATTN_WIDTH
        w_cat = jnp.concatenate(
            [w[:, :o], _dup_kv(w[:, o:o + KV_WIDTH]), _dup_kv(w[:, o + KV_WIDTH:o + 2 * KV_WIDTH]),
             w[:, o + 2 * KV_WIDTH:]], axis=1).astype(BF16)
        q, k, v, glu = _in_proj(xc, w_cat, pos, invf, ROW_TILE)
        att = _attention(sinks[i], q.reshape(b, s, -1), k.reshape(b, s, -1), v.reshape(b, s, -1))
        cnv = _conv_group(glu.reshape(b, s, -1), conv_w[i], vec(conv_b[i]),
                          vec(conv_ln_g[i]), vec(conv_ln_b[i]))
        wo = w_out[i].astype(BF16)
        x1, x1b, x1bt = _out_proj(att.reshape(t, -1), cnv.reshape(t, -1), xc, wo[:ATTN_WIDTH],
                            wo[ATTN_WIDTH:], vec(ln1_g[i]), vec(ln1_b[i]), ROW_TILE)
        keys = peer_keys[i].reshape(N_GROUPS, N_KEYS, PEER_HALF).astype(BF16)
        e1, c1, e2, r2 = _peer_routing(x1b, peer_query[i].astype(BF16), keys, ROUTE_TILE)
        peer = _peer_dense(x1bt, peer_u[i].astype(BF16), peer_v[i].astype(BF16).T, e1, c1, e2, r2,
                           DENSE_TOKENS, DENSE_EXPERTS)
        xc = _final(x1, peer, p[i].reshape(t, -1), ple_gate[i].astype(BF16),
                    ple_proj[i].astype(BF16), vec(ln2_g[i]), vec(ln2_b[i]), ROW_TILE)
    return xc.reshape(b, s, d)
```
